```python
import math
import jax, jax.numpy as jnp
from jax import lax
import numpy as np

D_MODEL = 1024
BATCH = 4
SEQ = 8192
DEPTH = 1
DEC_BATCH = 32
DEC_SEQ = 64
PAST_LEN = 2048

CHUNK = 64
SB_HEADS = 16
SB_HEAD_DIM = 64
SB_WIDTH = SB_HEADS * SB_HEAD_DIM
CONV_CH = D_MODEL
CONV_WIDTH = 31
MEM_TOKENS = 256
MEM_HEADS = 4
MEM_HEAD_DIM = D_MODEL // MEM_HEADS
MEM_WIDTH = MEM_HEADS * MEM_HEAD_DIM
N_BRANCH = 3
FFN_DIM = 2816
FFN_CONV_WIDTH = 3
Q_BLOCK = 128
NORM_EPS = 1e-6
IN_WIDTH = 3 * SB_WIDTH + 2 * CONV_CH + MEM_WIDTH + N_BRANCH * D_MODEL

kernel_name = 'stickbreak_conformer_memory_streaming_encoder'


def rms_norm(x, g):
    xf = x.astype(jnp.float32)
    y = xf * lax.rsqrt(jnp.mean(xf * xf, axis=-1, keepdims=True) + NORM_EPS)
    return (y * g.astype(jnp.float32)).astype(x.dtype)


def layer_norm(x, g, b):
    xf = x.astype(jnp.float32)
    mu = jnp.mean(xf, axis=-1, keepdims=True)
    var = jnp.mean(jnp.square(xf - mu), axis=-1, keepdims=True)
    y = (xf - mu) * lax.rsqrt(var + NORM_EPS)
    return (y * g.astype(jnp.float32) + b.astype(jnp.float32)).astype(x.dtype)


def causal_dwconv(xp, w):
    c = xp.shape[-1]
    return lax.conv_general_dilated(
        xp, w.astype(xp.dtype)[:, None, :], window_strides=(1,), padding='VALID',
        dimension_numbers=('NWC', 'WIO', 'NWC'), feature_group_count=c)


def _sb_block(qb, k, v, q_pos):
    k_pos = jnp.arange(k.shape[2], dtype=jnp.int32)
    mask = k_pos[None, :] < q_pos[:, None]
    z = jnp.einsum('bhqd,bhkd->bhqk', qb.astype(jnp.float32), k.astype(jnp.float32)) * (SB_HEAD_DIM ** -0.5)
    log_beta = jax.nn.log_sigmoid(z)
    log_1mb = jnp.where(mask, jax.nn.log_sigmoid(-z), 0.0)
    excl = lax.cumsum(log_1mb, axis=3, reverse=True) - log_1mb
    a = jnp.where(mask, jnp.exp(log_beta + excl), 0.0)
    return jnp.einsum('bhqk,bhkd->bhqd', a, v.astype(jnp.float32)).astype(qb.dtype)


def stick_breaking(q, k, v, past_len):
    b, h, t, d = q.shape
    if t <= Q_BLOCK:
        return _sb_block(q, k, v, past_len + jnp.arange(t, dtype=jnp.int32))
    nb = -(-t // Q_BLOCK)
    pad = nb * Q_BLOCK - t
    qp = jnp.pad(q, ((0, 0), (0, 0), (0, pad), (0, 0)))
    qb = qp.reshape(b, h, nb, Q_BLOCK, d).transpose(2, 0, 1, 3, 4)
    pos = (past_len + jnp.arange(nb * Q_BLOCK, dtype=jnp.int32)).reshape(nb, Q_BLOCK)
    out = lax.map(lambda a: _sb_block(a[0], k, v, a[1]), (qb, pos))
    return out.transpose(1, 2, 0, 3, 4).reshape(b, h, nb * Q_BLOCK, d)[:, :, :t]


def memory_kv(mem, g_mem, w_mem_kv):
    b, m, _ = mem.shape
    kv = rms_norm(mem, g_mem) @ w_mem_kv
    mk, mv = jnp.split(kv, 2, axis=-1)
    mk = mk.reshape(b, m, MEM_HEADS, MEM_HEAD_DIM).transpose(0, 2, 1, 3)
    mv = mv.reshape(b, m, MEM_HEADS, MEM_HEAD_DIM).transpose(0, 2, 1, 3)
    return mk, mv


def memory_attention(qm, mk, mv):
    s = jnp.einsum('bthd,bhmd->bhtm', qm.astype(jnp.float32), mk.astype(jnp.float32)) * (MEM_HEAD_DIM ** -0.5)
    p = jax.nn.softmax(s, axis=-1)
    return jnp.einsum('bhtm,bhmd->bthd', p, mv.astype(jnp.float32)).astype(qm.dtype)


def encoder_layer(x, mem_k, mem_v, sb_k_past, sb_v_past, conv_left, ffn_left, past_len,
                  g_mix_pre, g_mix_post, w_in, w_sb_o, conv_dw_w, conv_dw_b, conv_ln_g, conv_ln_b,
                  w_conv_o, w_mem_o, w_out, g_ffn_pre, g_ffn_post, w_ffn_up, ffn_dw_w, w_ffn_down):
    b, t, _ = x.shape
    h = rms_norm(x, g_mix_pre)
    proj = h @ w_in
    idx = [SB_WIDTH, 2 * SB_WIDTH, 3 * SB_WIDTH,
           3 * SB_WIDTH + CONV_CH, 3 * SB_WIDTH + 2 * CONV_CH,
           3 * SB_WIDTH + 2 * CONV_CH + MEM_WIDTH,
           3 * SB_WIDTH + 2 * CONV_CH + MEM_WIDTH + D_MODEL,
           3 * SB_WIDTH + 2 * CONV_CH + MEM_WIDTH + 2 * D_MODEL]
    q, k, v, ca, cb, qm, ga, gb, gc = jnp.split(proj, idx, axis=-1)

    def heads(z):
        return z.reshape(b, t, SB_HEADS, SB_HEAD_DIM).transpose(0, 2, 1, 3)
    q, k, v = heads(q), heads(k), heads(v)
    k_all = jnp.concatenate([sb_k_past.astype(k.dtype), k], axis=2)
    v_all = jnp.concatenate([sb_v_past.astype(v.dtype), v], axis=2)
    o_sb = stick_breaking(q, k_all, v_all, past_len)
    y_sb = o_sb.transpose(0, 2, 1, 3).reshape(b, t, SB_WIDTH) @ w_sb_o

    u = ca * jax.nn.sigmoid(cb)
    u_full = jnp.concatenate([conv_left.astype(u.dtype), u], axis=1)
    cc = causal_dwconv(u_full, conv_dw_w) + conv_dw_b
    cc = jax.nn.silu(layer_norm(cc, conv_ln_g, conv_ln_b))
    y_conv = cc @ w_conv_o

    o_mem = memory_attention(qm.reshape(b, t, MEM_HEADS, MEM_HEAD_DIM), mem_k, mem_v)
    y_mem = o_mem.reshape(b, t, MEM_WIDTH) @ w_mem_o

    merged = jax.nn.sigmoid(ga) * y_sb + jax.nn.sigmoid(gb) * y_conv + jax.nn.sigmoid(gc) * y_mem
    x = x + rms_norm(merged @ w_out, g_mix_post)

    h2 = rms_norm(x, g_ffn_pre)
    up = h2 @ w_ffn_up
    up_full = jnp.concatenate([ffn_left.astype(up.dtype), up], axis=1)
    upc = causal_dwconv(up_full, ffn_dw_w)
    fg, fv = jnp.split(upc, 2, axis=-1)
    x = x + rms_norm((jax.nn.gelu(fg) * fv) @ w_ffn_down, g_ffn_post)
    return (x, k, v, u_full[:, -(CONV_WIDTH - 1):], up_full[:, -(FFN_CONV_WIDTH - 1):])


def setup_inputs(seed: int = 0) -> dict:
    key = jax.random.key(seed)
    ks = iter(jax.random.split(key, 40))

    def nrm(shape, scale=1.0):
        return jax.random.normal(next(ks), shape, jnp.float32) * scale

    def gain(shape):
        return 1.0 + nrm(shape, 0.02)

    L = DEPTH
    return {
        'x_prompt': nrm((BATCH, SEQ, D_MODEL)),
        'x_sample': nrm((DEC_BATCH, DEC_SEQ, D_MODEL)),
        'mem_prompt': nrm((BATCH, MEM_TOKENS, D_MODEL)),
        'cache_sb_k': nrm((L, DEC_BATCH, SB_HEADS, PAST_LEN, SB_HEAD_DIM)),
        'cache_sb_v': nrm((L, DEC_BATCH, SB_HEADS, PAST_LEN, SB_HEAD_DIM)),
        'state_conv': nrm((L, DEC_BATCH, CONV_WIDTH - 1, CONV_CH), 0.5),
        'state_ffn_conv': nrm((L, DEC_BATCH, FFN_CONV_WIDTH - 1, 2 * FFN_DIM)),
        'cache_mem_k': nrm((L, DEC_BATCH, MEM_HEADS, MEM_TOKENS, MEM_HEAD_DIM)),
        'cache_mem_v': nrm((L, DEC_BATCH, MEM_HEADS, MEM_TOKENS, MEM_HEAD_DIM)),
        'g_mem': gain((L, D_MODEL)),
        'w_mem_kv': nrm((L, D_MODEL, 2 * MEM_WIDTH), D_MODEL ** -0.5),
        'g_mix_pre': gain((L, D_MODEL)),
        'g_mix_post': gain((L, D_MODEL)),
        'w_in': nrm((L, D_MODEL, IN_WIDTH), D_MODEL ** -0.5),
        'w_sb_o': nrm((L, SB_WIDTH, D_MODEL), SB_WIDTH ** -0.5),
        'conv_dw_w': nrm((L, CONV_WIDTH, CONV_CH), CONV_WIDTH ** -0.5),
        'conv_dw_b': nrm((L, CONV_CH), 0.02),
        'conv_ln_g': gain((L, CONV_CH)),
        'conv_ln_b': nrm((L, CONV_CH), 0.02),
        'w_conv_o': nrm((L, CONV_CH, D_MODEL), CONV_CH ** -0.5),
        'w_mem_o': nrm((L, MEM_WIDTH, D_MODEL), MEM_WIDTH ** -0.5),
        'w_out': nrm((L, D_MODEL, D_MODEL), D_MODEL ** -0.5),
        'g_ffn_pre': gain((L, D_MODEL)),
        'g_ffn_post': gain((L, D_MODEL)),
        'w_ffn_up': nrm((L, D_MODEL, 2 * FFN_DIM), D_MODEL ** -0.5),
        'ffn_dw_w': nrm((L, FFN_CONV_WIDTH, 2 * FFN_DIM), FFN_CONV_WIDTH ** -0.5),
        'w_ffn_down': nrm((L, FFN_DIM, D_MODEL), FFN_DIM ** -0.5),
    }


def reference(x_prompt, x_sample, mem_prompt, cache_sb_k, cache_sb_v, state_conv, state_ffn_conv,
              cache_mem_k, cache_mem_v, g_mem, w_mem_kv, g_mix_pre, g_mix_post, w_in, w_sb_o,
              conv_dw_w, conv_dw_b, conv_ln_g, conv_ln_b, w_conv_o, w_mem_o, w_out,
              g_ffn_pre, g_ffn_post, w_ffn_up, ffn_dw_w, w_ffn_down):
    yp, ys = x_prompt, x_sample
    bp = x_prompt.shape[0]
    past_len = cache_sb_k.shape[3]
    kp_l, vp_l, ks_l, vs_l, cp_l, cs_l, fp_l, fs_l, mkp_l, mvp_l = ([] for _ in range(10))
    for l in range(DEPTH):
        lw = (g_mix_pre[l], g_mix_post[l], w_in[l], w_sb_o[l], conv_dw_w[l], conv_dw_b[l],
              conv_ln_g[l], conv_ln_b[l], w_conv_o[l], w_mem_o[l], w_out[l],
              g_ffn_pre[l], g_ffn_post[l], w_ffn_up[l], ffn_dw_w[l], w_ffn_down[l])
        mk_p, mv_p = memory_kv(mem_prompt, g_mem[l], w_mem_kv[l])
        empty = jnp.zeros((bp, SB_HEADS, 0, SB_HEAD_DIM), yp.dtype)
        yp, kp, vp, cp, fp = encoder_layer(
            yp, mk_p, mv_p, empty, empty,
            jnp.zeros((bp, CONV_WIDTH - 1, CONV_CH), yp.dtype),
            jnp.zeros((bp, FFN_CONV_WIDTH - 1, 2 * FFN_DIM), yp.dtype), 0, *lw)
        ys, k_s, v_s, c_s, f_s = encoder_layer(
            ys, cache_mem_k[l], cache_mem_v[l], cache_sb_k[l], cache_sb_v[l],
            state_conv[l], state_ffn_conv[l], past_len, *lw)
        kp_l.append(kp); vp_l.append(vp); ks_l.append(k_s); vs_l.append(v_s)
        cp_l.append(cp); cs_l.append(c_s); fp_l.append(fp); fs_l.append(f_s)
        mkp_l.append(mk_p); mvp_l.append(mv_p)
    return (yp, ys, jnp.stack(kp_l), jnp.stack(vp_l), jnp.stack(ks_l), jnp.stack(vs_l),
            jnp.stack(cp_l), jnp.stack(cs_l), jnp.stack(fp_l), jnp.stack(fs_l),
            jnp.stack(mkp_l), jnp.stack(mvp_l))
```

```python
import functools

import jax
import jax.numpy as jnp
from jax import lax
from jax.experimental import pallas as pl
from jax.experimental.pallas import tpu as pltpu

F32 = jnp.float32
BF16 = jnp.bfloat16

D_MODEL = 1024
SB_HEADS = 16
SB_HEAD_DIM = 64
MEM_HEADS = 4
MEM_HEAD_DIM = 256
MEM_TOKENS = 256
CONV_WIDTH = 31
FFN_DIM = 2816
FFN_CONV_WIDTH = 3
NORM_EPS = 1e-6
N_IN_GROUPS = 9

SUBLANES = 8
CONV_HALO = 32
VMEM_LIMIT_BYTES = 56 * 1024 * 1024

SB_DEAD_LOG = -110.0
SB_BLOCK = 256


def _rms(x, g):
    return x * lax.rsqrt(jnp.mean(x * x, axis=-1, keepdims=True) + NORM_EPS) * g


def _sigmoid(x):
    return 1.0 / (1.0 + jnp.exp(-x))


def _dot(a, b):
    return jnp.dot(a, b, preferred_element_type=F32)


def _dot_nt(a, b):
    return lax.dot_general(a, b, (((1,), (1,)), ((), ())), preferred_element_type=F32)


def _params(sem):
    return pltpu.CompilerParams(dimension_semantics=sem, vmem_limit_bytes=VMEM_LIMIT_BYTES)


def _memkv_kernel(mem_ref, g_ref, w_ref, o_ref, ob_ref, hb_ref):
    @pl.when(pl.program_id(1) == 0)
    def _():
        hb_ref[...] = _rms(mem_ref[0], g_ref[...]).astype(BF16)

    res = _dot(hb_ref[...], w_ref[...])
    o_ref[0, 0, 0] = res
    ob_ref[0, 0, 0] = res.astype(BF16)


def _memkv(mem, g, w_bf):
    b = mem.shape[0]
    nj = 2 * MEM_HEADS
    out_map = lambda i, j: (j // MEM_HEADS, i, j % MEM_HEADS, 0, 0)
    shape = (2, b, MEM_HEADS, MEM_TOKENS, MEM_HEAD_DIM)
    return pl.pallas_call(
        _memkv_kernel,
        grid=(b, nj),
        in_specs=[
            pl.BlockSpec((1, MEM_TOKENS, D_MODEL), lambda i, j: (i, 0, 0)),
            pl.BlockSpec((1, D_MODEL), lambda i, j: (0, 0)),
            pl.BlockSpec((D_MODEL, MEM_HEAD_DIM), lambda i, j: (0, j)),
        ],
        out_specs=[
            pl.BlockSpec((1, 1, 1, MEM_TOKENS, MEM_HEAD_DIM), out_map),
            pl.BlockSpec((1, 1, 1, MEM_TOKENS, MEM_HEAD_DIM), out_map),
        ],
        out_shape=[jax.ShapeDtypeStruct(shape, F32), jax.ShapeDtypeStruct(shape, BF16)],
        scratch_shapes=[pltpu.VMEM((MEM_TOKENS, D_MODEL), BF16)],
        compiler_params=_params(("arbitrary", "arbitrary")),
        name="memkv",
    )(mem, g, w_bf)


def _inproj_kernel(x_ref, g_ref, w_ref, q_ref, k_ref, v_ref, kf_ref, vf_ref, u_ref, qm_ref,
                   ga_ref, gb_ref, gc_ref, hb_ref, ca_ref, *, G, tt):
    j = pl.program_id(2)
    rows = G * tt

    @pl.when(j == 0)
    def _():
        hb_ref[...] = _rms(x_ref[...].reshape(rows, D_MODEL), g_ref[...]).astype(BF16)

    res = _dot(hb_ref[...], w_ref[...])

    def tok(ref, val):
        ref[...] = val.reshape(G, tt, D_MODEL).astype(ref.dtype)

    def heads(ref):
        for g in range(G):
            for h in range(SB_HEADS):
                ref[g, h] = res[g * tt:(g + 1) * tt, h * SB_HEAD_DIM:(h + 1) * SB_HEAD_DIM]

    @pl.when(j == 0)
    def _():
        tok(q_ref, res * (SB_HEAD_DIM ** -0.5))

    @pl.when(j == 1)
    def _():
        tok(k_ref, res)
        heads(kf_ref)

    @pl.when(j == 2)
    def _():
        tok(v_ref, res)
        heads(vf_ref)

    @pl.when(j == 3)
    def _():
        ca_ref[...] = res

    @pl.when(j == 4)
    def _():
        tok(u_ref, ca_ref[...] * _sigmoid(res))

    @pl.when(j == 5)
    def _():
        tok(qm_ref, res * (MEM_HEAD_DIM ** -0.5))

    @pl.when(j == 6)
    def _():
        tok(ga_ref, _sigmoid(res))

    @pl.when(j == 7)
    def _():
        tok(gb_ref, _sigmoid(res))

    @pl.when(j == 8)
    def _():
        tok(gc_ref, _sigmoid(res))


def _inproj(x, g, w_bf, *, G, tt):
    b, t, _ = x.shape
    rows = G * tt
    tok_spec = pl.BlockSpec((G, tt, D_MODEL), lambda i, s, j: (i, s, 0))
    head_spec = pl.BlockSpec((G, SB_HEADS, tt, SB_HEAD_DIM), lambda i, s, j: (i, 0, s, 0))
    tok_bf = jax.ShapeDtypeStruct((b, t, D_MODEL), BF16)
    tok_f32 = jax.ShapeDtypeStruct((b, t, D_MODEL), F32)
    head_f32 = jax.ShapeDtypeStruct((b, SB_HEADS, t, SB_HEAD_DIM), F32)
    return pl.pallas_call(
        functools.partial(_inproj_kernel, G=G, tt=tt),
        grid=(b // G, t // tt, N_IN_GROUPS),
        in_specs=[
            tok_spec,
            pl.BlockSpec((1, D_MODEL), lambda i, s, j: (0, 0)),
            pl.BlockSpec((D_MODEL, D_MODEL), lambda i, s, j: (0, j)),
        ],
        out_specs=[tok_spec, tok_spec, tok_spec, head_spec, head_spec, tok_spec, tok_spec,
                   tok_spec, tok_spec, tok_spec],
        out_shape=[tok_bf, tok_bf, tok_bf, head_f32, head_f32, tok_f32, tok_bf, tok_bf, tok_bf, tok_bf],
        scratch_shapes=[pltpu.VMEM((rows, D_MODEL), BF16), pltpu.VMEM((rows, D_MODEL), F32)],
        compiler_params=_params(("arbitrary", "arbitrary", "arbitrary")),
        name="inproj",
    )(x, g, w_bf)


def _neg_upper(n):
    j = lax.broadcasted_iota(jnp.int32, (n, n), 0)
    s = lax.broadcasted_iota(jnp.int32, (n, n), 1)
    return jnp.where(j > s, -1.0, 0.0).astype(BF16)


def _sb_scores(q, k, nu, mask):
    z = _dot_nt(q, k)
    nl = jnp.maximum(z, 0.0) + jnp.log(1.0 + jnp.exp(-jnp.abs(z)))
    log_beta = z - nl
    if mask is not None:
        nl = jnp.where(mask, nl, 0.0)
    hi = nl.astype(BF16)
    lo = (nl - hi.astype(F32)).astype(BF16)
    excl = _dot(hi, nu) + _dot(lo, nu)
    tot = excl[:, 0:1] - nl[:, 0:1]
    return log_beta, excl, tot


def _sb_weights(log_beta, excl, c, mask):
    a = jnp.exp(log_beta + excl + c)
    if mask is not None:
        a = jnp.where(mask, a, 0.0)
    return a.astype(BF16)


def _strict_lower(n):
    r = lax.broadcasted_iota(jnp.int32, (n, n), 0)
    s = lax.broadcasted_iota(jnp.int32, (n, n), 1)
    return s < r


def _sb_prompt_kernel(q_ref, k_ref, v_ref, nu_ref, o_ref, acc_ref, c_ref, *, tq):
    qi = pl.program_id(2)
    nu = nu_ref[...]
    mask = _strict_lower(tq)
    q = q_ref[0]
    hsl = [slice(h * SB_HEAD_DIM, (h + 1) * SB_HEAD_DIM) for h in range(2)]

    def kv(kb):
        start = pl.multiple_of(kb * tq, tq)
        return k_ref[0, pl.ds(start, tq), :], v_ref[0, pl.ds(start, tq), :]

    def cmax():
        return jnp.maximum(jnp.max(c_ref[0]), jnp.max(c_ref[1]))

    @pl.when(qi == 0)
    def _():
        kd, vd = kv(qi)
        for h in range(2):
            lb, ex, tot = _sb_scores(q[:, hsl[h]], kd[:, hsl[h]], nu, mask)
            acc_ref[h] = _dot(_sb_weights(lb, ex, 0.0, mask), vd[:, hsl[h]])
            c_ref[h] = tot

    @pl.when(qi > 0)
    def _():
        kd, vd = kv(qi)
        kp, vp = kv(qi - 1)
        for h in range(2):
            lb, ex, tot = _sb_scores(q[:, hsl[h]], kd[:, hsl[h]], nu, mask)
            lbp, exp_, totp = _sb_scores(q[:, hsl[h]], kp[:, hsl[h]], nu, None)
            o = _dot(_sb_weights(lb, ex, 0.0, mask), vd[:, hsl[h]])
            o = o + _dot(_sb_weights(lbp, exp_, tot, None), vp[:, hsl[h]])
            acc_ref[h] = o
            c_ref[h] = tot + totp

    def cond(carry):
        kb, cm = carry
        return jnp.logical_and(kb >= 0, cm > SB_DEAD_LOG)

    def body(carry):
        kb, _ = carry
        kk, vv = kv(kb)
        for h in range(2):
            lb, ex, tot = _sb_scores(q[:, hsl[h]], kk[:, hsl[h]], nu, None)
            c = c_ref[h]
            acc_ref[h] += _dot(_sb_weights(lb, ex, c, None), vv[:, hsl[h]])
            c_ref[h] = c + tot
        return kb - 1, cmax()

    lax.while_loop(cond, body, (qi - 2, cmax()))
    o_ref[0] = jnp.concatenate([acc_ref[0], acc_ref[1]], axis=1).astype(BF16)


def _sb_prompt(q, k, v, *, tq):
    b, t, _ = q.shape
    hp = SB_HEADS // 2
    lanes = 2 * SB_HEAD_DIM
    nu = _neg_upper(tq)
    return pl.pallas_call(
        functools.partial(_sb_prompt_kernel, tq=tq),
        grid=(b, hp, t // tq),
        in_specs=[
            pl.BlockSpec((1, tq, lanes), lambda i, h, s: (i, s, h)),
            pl.BlockSpec((1, t, lanes), lambda i, h, s: (i, 0, h)),
            pl.BlockSpec((1, t, lanes), lambda i, h, s: (i, 0, h)),
            pl.BlockSpec((tq, tq), lambda i, h, s: (0, 0)),
        ],
        out_specs=pl.BlockSpec((1, tq, lanes), lambda i, h, s: (i, s, h)),
        out_shape=jax.ShapeDtypeStruct((b, t, D_MODEL), BF16),
        scratch_shapes=[pltpu.VMEM((2, tq, SB_HEAD_DIM), F32), pltpu.VMEM((2, tq, 1), F32)],
        compiler_params=_params(("arbitrary", "arbitrary", "arbitrary")),
        name="sb_prompt",
    )(q, k, v, nu)


def _sb_sample_kernel(q_ref, kn_ref, vn_ref, kc_ref, vc_ref, nun_ref, nup_ref, o_ref, *, t, past, tk):
    mask = _strict_lower(t)
    nun = nun_ref[...]
    nup = nup_ref[...]
    outs = []
    for h in range(2):
        sl = slice(h * SB_HEAD_DIM, (h + 1) * SB_HEAD_DIM)
        qh = q_ref[0][:, sl]
        lb, ex, tot = _sb_scores(qh, kn_ref[0][:, sl], nun, mask)
        o = _dot(_sb_weights(lb, ex, 0.0, mask), vn_ref[0][:, sl])
        c = tot
        for kb in range(past // tk - 1, -1, -1):
            kk = kc_ref[0, h, kb * tk:(kb + 1) * tk, :].astype(BF16)
            vv = vc_ref[0, h, kb * tk:(kb + 1) * tk, :].astype(BF16)
            lb, ex, tot = _sb_scores(qh, kk, nup, None)
            o = o + _dot(_sb_weights(lb, ex, c, None), vv)
            c = c + tot
        outs.append(o)
    o_ref[0] = jnp.concatenate(outs, axis=1).astype(BF16)


def _sb_sample(q, kn, vn, kc, vc, *, tk):
    b, t, _ = q.shape
    past = kc.shape[2]
    hp = SB_HEADS // 2
    lanes = 2 * SB_HEAD_DIM
    new_spec = pl.BlockSpec((1, t, lanes), lambda i, h: (i, 0, h))
    cache_spec = pl.BlockSpec((1, 2, past, SB_HEAD_DIM), lambda i, h: (i, h, 0, 0))
    return pl.pallas_call(
        functools.partial(_sb_sample_kernel, t=t, past=past, tk=tk),
        grid=(b, hp),
        in_specs=[new_spec, new_spec, new_spec, cache_spec, cache_spec,
                  pl.BlockSpec((t, t), lambda i, h: (0, 0)),
                  pl.BlockSpec((tk, tk), lambda i, h: (0, 0))],
        out_specs=new_spec,
        out_shape=jax.ShapeDtypeStruct((b, t, D_MODEL), BF16),
        compiler_params=_params(("arbitrary", "arbitrary")),
        name="sb_sample",
    )(q, kn, vn, kc, vc, _neg_upper(t), _neg_upper(tk))


CONV_ROWS = 64
CONV_LANES = 256


def _mix_kernel(x_ref, osb_ref, u_ref, left_ref, qm_ref, ga_ref, gb_ref, gc_ref, mk_ref, mv_ref,
                wsb_ref, wcv_ref, wmo_ref, wout_ref, dww_ref, dwb_ref, lng_ref, lnb_ref, gpost_ref,
                y_ref, uf_ref, cc_ref, om_ref, *, G, tt):
    s = pl.program_id(1)
    rows = G * tt
    H0 = CONV_HALO
    off = H0 - (CONV_WIDTH - 1)

    @pl.when(s == 0)
    def _():
        uf_ref[:, 0:H0, :] = left_ref[...]

    @pl.when(s > 0)
    def _():
        uf_ref[:, 0:H0, :] = uf_ref[:, tt:tt + H0, :]

    uf_ref[:, H0:H0 + tt, :] = u_ref[...]

    rc = min(CONV_ROWS, tt)
    span = rc + (CONV_WIDTH - 1) // SUBLANES * SUBLANES
    for g in range(G):
        for r0 in range(0, tt, rc):
            for c0 in range(0, D_MODEL, CONV_LANES):
                cs = slice(c0, c0 + CONV_LANES)
                acc = jnp.broadcast_to(dwb_ref[:, cs], (rc, CONV_LANES))
                for r in range(SUBLANES):
                    base = uf_ref[g, r0 + off + r:r0 + off + r + span, cs]
                    for m in range(0, CONV_WIDTH - r, SUBLANES):
                        acc = acc + base[m:m + rc] * dww_ref[r + m:r + m + 1, cs]
                cc_ref[g * tt + r0:g * tt + r0 + rc, cs] = acc

    cc = cc_ref[...]
    mu = jnp.mean(cc, axis=-1, keepdims=True)
    d = cc - mu
    var = jnp.mean(d * d, axis=-1, keepdims=True)
    yn = d * lax.rsqrt(var + NORM_EPS) * lng_ref[...] + lnb_ref[...]
    y_conv = _dot((yn * _sigmoid(yn)).astype(BF16), wcv_ref[...])

    y_sb = _dot(osb_ref[...].reshape(rows, D_MODEL), wsb_ref[...])

    for g in range(G):
        for h in range(MEM_HEADS):
            cs = slice(h * MEM_HEAD_DIM, (h + 1) * MEM_HEAD_DIM)
            sc = _dot_nt(qm_ref[g, :, cs], mk_ref[g, h].astype(BF16))
            p = jnp.exp(sc - jnp.max(sc, axis=-1, keepdims=True))
            den = jnp.sum(p, axis=-1, keepdims=True)
            oh = _dot(p.astype(BF16), mv_ref[g, h].astype(BF16)) * (1.0 / den)
            om_ref[g * tt:(g + 1) * tt, cs] = oh.astype(BF16)
    y_mem = _dot(om_ref[...], wmo_ref[...])

    ga = ga_ref[...].reshape(rows, D_MODEL).astype(F32)
    gb = gb_ref[...].reshape(rows, D_MODEL).astype(F32)
    gc = gc_ref[...].reshape(rows, D_MODEL).astype(F32)
    merged = ga * y_sb + gb * y_conv + gc * y_mem
    proj = _dot(merged.astype(BF16), wout_ref[...])
    y = x_ref[...].reshape(rows, D_MODEL) + _rms(proj, gpost_ref[...])
    y_ref[...] = y.reshape(G, tt, D_MODEL)


def _mix(x, osb, u, left, qm, ga, gb, gc, mk, mv, wsb, wcv, wmo, wout, dww, dwb, lng, lnb, gpost, *, G, tt):
    b, t, _ = x.shape
    rows = G * tt
    tok = pl.BlockSpec((G, tt, D_MODEL), lambda i, s: (i, s, 0))
    mem = pl.BlockSpec((G, MEM_HEADS, MEM_TOKENS, MEM_HEAD_DIM), lambda i, s: (i, 0, 0, 0))
    wsq = pl.BlockSpec((D_MODEL, D_MODEL), lambda i, s: (0, 0))
    vec = pl.BlockSpec((1, D_MODEL), lambda i, s: (0, 0))
    return pl.pallas_call(
        functools.partial(_mix_kernel, G=G, tt=tt),
        grid=(b // G, t // tt),
        in_specs=[tok, tok, tok,
                  pl.BlockSpec((G, CONV_HALO, D_MODEL), lambda i, s: (i, 0, 0)),
                  tok, tok, tok, tok, mem, mem, wsq, wsq, wsq, wsq,
                  pl.BlockSpec((CONV_WIDTH, D_MODEL), lambda i, s: (0, 0)),
                  vec, vec, vec, vec],
        out_specs=tok,
        out_shape=jax.ShapeDtypeStruct((b, t, D_MODEL), F32),
        scratch_shapes=[pltpu.VMEM((G, CONV_HALO + tt, D_MODEL), F32),
                        pltpu.VMEM((rows, D_MODEL), F32),
                        pltpu.VMEM((rows, D_MODEL), BF16)],
        compiler_params=_params(("arbitrary", "arbitrary")),
        name="mix",
    )(x, osb, u, left, qm, ga, gb, gc, mk, mv, wsb, wcv, wmo, wout, dww, dwb, lng, lnb, gpost)


def _gelu_tanh(x):
    return 0.5 * x * (1.0 + jnp.tanh(0.7978845608028654 * (x + 0.044715 * (x * x * x))))


def _ffn_kernel(x_ref, lg_ref, lv_ref, gpre_ref, gpost_ref, wg_ref, wv_ref, dwg_ref, dwv_ref, wd_ref,
                y_ref, st_ref, hb_ref, acc_ref, upf_ref, carry_ref, act_ref, *, G, tt, nf):
    s = pl.program_id(1)
    f = pl.program_id(2)
    rows = G * tt
    S = SUBLANES

    @pl.when(f == 0)
    def _():
        hb_ref[...] = _rms(x_ref[...].reshape(rows, D_MODEL), gpre_ref[...]).astype(BF16)

    hb = hb_ref[...]
    ups = (_dot(hb, wg_ref[...]), _dot(hb, wv_ref[...]))
    lefts = (lg_ref, lv_ref)
    dws = (dwg_ref, dwv_ref)

    for half in range(2):
        @pl.when(s == 0)
        def _():
            upf_ref[half, :, 0:S, :] = lefts[half][...]

        @pl.when(s > 0)
        def _():
            upf_ref[half, :, 0:S, :] = carry_ref[f, half]

        upf_ref[half, :, S:S + tt, :] = ups[half].reshape(G, tt, -1)
        last = upf_ref[half, :, tt:tt + S, :]
        carry_ref[f, half] = last
        st_ref[half, f] = last

    for g in range(G):
        conv = []
        for half in range(2):
            dw = dws[half]
            c = (upf_ref[half, g, S:S + tt, :] * dw[2:3, :]
                 + upf_ref[half, g, S - 1:S - 1 + tt, :] * dw[1:2, :]
                 + upf_ref[half, g, S - 2:S - 2 + tt, :] * dw[0:1, :])
            conv.append(c)
        act_ref[g * tt:(g + 1) * tt, :] = (_gelu_tanh(conv[0]) * conv[1]).astype(BF16)

    part = _dot(act_ref[...], wd_ref[...])

    @pl.when(f == 0)
    def _():
        acc_ref[...] = part

    @pl.when(f > 0)
    def _():
        acc_ref[...] += part

    @pl.when(f == nf - 1)
    def _():
        y = x_ref[...].reshape(rows, D_MODEL) + _rms(acc_ref[...], gpost_ref[...])
        y_ref[...] = y.reshape(G, tt, D_MODEL)


def _ffn(x, left, gpre, gpost, wup, dw, wdown, *, G, tt, nf):
    b, t, _ = x.shape
    rows = G * tt
    tf = FFN_DIM // nf
    S = SUBLANES
    tok = pl.BlockSpec((G, tt, D_MODEL), lambda i, s, f: (i, s, 0))
    vec = pl.BlockSpec((1, D_MODEL), lambda i, s, f: (0, 0))
    y, st = pl.pallas_call(
        functools.partial(_ffn_kernel, G=G, tt=tt, nf=nf),
        grid=(b // G, t // tt, nf),
        in_specs=[tok,
                  pl.BlockSpec((G, S, tf), lambda i, s, f: (i, 0, f)),
                  pl.BlockSpec((G, S, tf), lambda i, s, f: (i, 0, nf + f)),
                  vec, vec,
                  pl.BlockSpec((D_MODEL, tf), lambda i, s, f: (0, f)),
                  pl.BlockSpec((D_MODEL, tf), lambda i, s, f: (0, nf + f)),
                  pl.BlockSpec((FFN_CONV_WIDTH, tf), lambda i, s, f: (0, f)),
                  pl.BlockSpec((FFN_CONV_WIDTH, tf), lambda i, s, f: (0, nf + f)),
                  pl.BlockSpec((tf, D_MODEL), lambda i, s, f: (f, 0))],
        out_specs=[tok, pl.BlockSpec((2, nf, G, S, tf), lambda i, s, f: (0, 0, i, 0, 0))],
        out_shape=[jax.ShapeDtypeStruct((b, t, D_MODEL), F32),
                   jax.ShapeDtypeStruct((2, nf, b, S, tf), F32)],
        scratch_shapes=[pltpu.VMEM((rows, D_MODEL), BF16),
                        pltpu.VMEM((rows, D_MODEL), F32),
                        pltpu.VMEM((2, G, S + tt, tf), F32),
                        pltpu.VMEM((nf, 2, G, S, tf), F32),
                        pltpu.VMEM((rows, tf), BF16)],
        compiler_params=_params(("arbitrary", "arbitrary", "arbitrary")),
        name="ffn",
    )(x, left, left, gpre, gpost, wup, wup, dw, dw, wdown)
    keep = FFN_CONV_WIDTH - 1
    state = st[:, :, :, S - keep:, :].transpose(2, 3, 0, 1, 4).reshape(b, keep, 2 * FFN_DIM)
    return y, state


def _layer(x, mk, mv, sb_cache, conv_left, ffn_left, w, *, G_in, tt_in, G_mix, tt_mix, G_ffn, tt_ffn, nf):
    b, t, _ = x.shape
    q, k, v, kf, vf, u, qm, ga, gb, gc = _inproj(x, w["g_mix_pre"], w["w_in"], G=G_in, tt=tt_in)
    if sb_cache is None:
        osb = _sb_prompt(q, k, v, tq=SB_BLOCK)
    else:
        osb = _sb_sample(q, k, v, sb_cache[0], sb_cache[1], tk=SB_BLOCK)

    keep = CONV_WIDTH - 1
    left = jnp.pad(conv_left, ((0, 0), (CONV_HALO - keep, 0), (0, 0)))
    x1 = _mix(x, osb, u, left, qm, ga, gb, gc, mk, mv, w["w_sb_o"], w["w_conv_o"], w["w_mem_o"], w["w_out"],
              w["conv_dw_w"], w["conv_dw_b"], w["conv_ln_g"], w["conv_ln_b"], w["g_mix_post"],
              G=G_mix, tt=tt_mix)
    conv_state = jnp.concatenate([conv_left, u], axis=1)[:, -keep:] if t < keep else u[:, t - keep:]

    fkeep = FFN_CONV_WIDTH - 1
    fleft = jnp.pad(ffn_left, ((0, 0), (SUBLANES - fkeep, 0), (0, 0)))
    y, ffn_state = _ffn(x1, fleft, w["g_ffn_pre"], w["g_ffn_post"], w["w_ffn_up"], w["ffn_dw_w"],
                        w["w_ffn_down"], G=G_ffn, tt=tt_ffn, nf=nf)
    return y, kf, vf, conv_state, ffn_state


def kernel(x_prompt, x_sample, mem_prompt, cache_sb_k, cache_sb_v, state_conv, state_ffn_conv, cache_mem_k, cache_mem_v, g_mem, w_mem_kv, g_mix_pre, g_mix_post, w_in, w_sb_o, conv_dw_w, conv_dw_b, conv_ln_g, conv_ln_b, w_conv_o, w_mem_o, w_out, g_ffn_pre, g_ffn_post, w_ffn_up, ffn_dw_w, w_ffn_down):
    depth = w_in.shape[0]
    bp = x_prompt.shape[0]
    yp, ys = x_prompt, x_sample
    outs = [[] for _ in range(10)]
    for l in range(depth):
        vec = lambda a: a[l].reshape(1, -1)
        w = {
            "g_mix_pre": vec(g_mix_pre), "g_mix_post": vec(g_mix_post),
            "w_in": w_in[l].astype(BF16), "w_sb_o": w_sb_o[l].astype(BF16),
            "conv_dw_w": conv_dw_w[l], "conv_dw_b": vec(conv_dw_b),
            "conv_ln_g": vec(conv_ln_g), "conv_ln_b": vec(conv_ln_b),
            "w_conv_o": w_conv_o[l].astype(BF16), "w_mem_o": w_mem_o[l].astype(BF16),
            "w_out": w_out[l].astype(BF16),
            "g_ffn_pre": vec(g_ffn_pre), "g_ffn_post": vec(g_ffn_post),
            "w_ffn_up": w_ffn_up[l].astype(BF16), "ffn_dw_w": ffn_dw_w[l],
            "w_ffn_down": w_ffn_down[l].astype(BF16),
        }
        mkv, mkv_bf = _memkv(mem_prompt, vec(g_mem), w_mem_kv[l].astype(BF16))
        yp, kp, vp, cp, fp = _layer(
            yp, mkv_bf[0], mkv_bf[1], None,
            jnp.zeros((bp, CONV_WIDTH - 1, D_MODEL), F32),
            jnp.zeros((bp, FFN_CONV_WIDTH - 1, 2 * FFN_DIM), F32), w,
            G_in=1, tt_in=512, G_mix=1, tt_mix=256, G_ffn=1, tt_ffn=512, nf=2)
        ys, ks, vs, cs, fs = _layer(
            ys, cache_mem_k[l], cache_mem_v[l], (cache_sb_k[l], cache_sb_v[l]),
            state_conv[l], state_ffn_conv[l], w,
            G_in=8, tt_in=ys.shape[1], G_mix=4, tt_mix=ys.shape[1], G_ffn=8, tt_ffn=ys.shape[1], nf=2)
        for lst, val in zip(outs, (kp, vp, ks, vs, cp, cs, fp, fs, mkv[0], mkv[1])):
            lst.append(val)
    return (yp, ys) + tuple(jnp.stack(o) for o in outs)
```

```python
import functools

import jax
import jax.numpy as jnp
from jax import lax
from jax.experimental import pallas as pl
from jax.experimental.pallas import tpu as pltpu

F32 = jnp.float32
BF16 = jnp.bfloat16

D_MODEL = 1024
SB_HEADS = 16
SB_HEAD_DIM = 64
MEM_HEADS = 4
MEM_HEAD_DIM = 256
MEM_TOKENS = 256
CONV_WIDTH = 31
FFN_DIM = 2816
FFN_CONV_WIDTH = 3
NORM_EPS = 1e-6
N_IN_GROUPS = 9

SUBLANES = 8
LANES = 128
CONV_HALO = 32
VMEM_LIMIT_BYTES = 56 * 1024 * 1024

SB_DEAD_LOG = -110.0
SB_MASKED_SCORE = -1e4
SB_BLOCK = 256
SB_PROMPT_HEADS = 8
SB_SAMPLE_HEADS = 8
SB_SAMPLE_NEAR = 512


def _rms(x, g):
    return x * lax.rsqrt(jnp.mean(x * x, axis=-1, keepdims=True) + NORM_EPS) * g


def _sigmoid(x):
    return 0.5 * jnp.tanh(0.5 * x) + 0.5


def _softplus(z):
    sign = jnp.uint32(0x80000000)
    neg_abs = lax.bitcast_convert_type(lax.bitcast_convert_type(z, jnp.uint32) | sign, F32)
    return jnp.maximum(z, 0.0) + jnp.log(1.0 + jnp.exp(neg_abs))


def _dot(a, b):
    return jnp.dot(a, b, preferred_element_type=F32)


def _dot_nt(a, b):
    return lax.dot_general(a, b, (((1,), (1,)), ((), ())), preferred_element_type=F32)


def _params(sem):
    return pltpu.CompilerParams(dimension_semantics=sem, vmem_limit_bytes=VMEM_LIMIT_BYTES)


def _memkv_kernel(mem_ref, g_ref, w_ref, o_ref, ob_ref, hb_ref):
    @pl.when(pl.program_id(1) == 0)
    def _():
        hb_ref[...] = _rms(mem_ref[0], g_ref[...]).astype(BF16)

    res = _dot(hb_ref[...], w_ref[...])
    o_ref[0, 0, 0] = res
    ob_ref[0, 0, 0] = res.astype(BF16)


def _memkv(mem, g, w_bf):
    b = mem.shape[0]
    nj = 2 * MEM_HEADS
    out_map = lambda i, j: (j // MEM_HEADS, i, j % MEM_HEADS, 0, 0)
    shape = (2, b, MEM_HEADS, MEM_TOKENS, MEM_HEAD_DIM)
    return pl.pallas_call(
        _memkv_kernel,
        grid=(b, nj),
        in_specs=[
            pl.BlockSpec((1, MEM_TOKENS, D_MODEL), lambda i, j: (i, 0, 0)),
            pl.BlockSpec((1, D_MODEL), lambda i, j: (0, 0)),
            pl.BlockSpec((D_MODEL, MEM_HEAD_DIM), lambda i, j: (0, j)),
        ],
        out_specs=[
            pl.BlockSpec((1, 1, 1, MEM_TOKENS, MEM_HEAD_DIM), out_map),
            pl.BlockSpec((1, 1, 1, MEM_TOKENS, MEM_HEAD_DIM), out_map),
        ],
        out_shape=[jax.ShapeDtypeStruct(shape, F32), jax.ShapeDtypeStruct(shape, BF16)],
        scratch_shapes=[pltpu.VMEM((MEM_TOKENS, D_MODEL), BF16)],
        compiler_params=_params(("arbitrary", "arbitrary")),
        name="memkv",
    )(mem, g, w_bf)


INPROJ_COLS = 512

INPROJ_OUT_GROUP = (0, 1, 2, 1, 2, 4, 5, 6, 7, 8)


def _inproj_kernel(x_ref, g_ref, w_ref, q_ref, k_ref, v_ref, kf_ref, vf_ref, u_ref, qm_ref,
                   ga_ref, gb_ref, gc_ref, hb_ref, ca_ref, *, G, tt, dmajor):
    j = pl.program_id(2)
    rows = G * tt
    cc = INPROJ_COLS

    @pl.when(j == 0)
    def _():
        hb_ref[...] = _rms(x_ref[...].reshape(rows, D_MODEL), g_ref[...]).astype(BF16)

    def group(epilogue):
        for n0 in range(0, D_MODEL, cc):
            epilogue(n0, _dot(hb_ref[...], w_ref[:, n0:n0 + cc]))

    def tok(ref, n0, val):
        ref[:, :, n0:n0 + cc] = val.reshape(G, tt, cc).astype(ref.dtype)

    def heads(ref, n0, res):
        if dmajor:
            ref[0, n0:n0 + cc, :] = res.T
        else:
            for g in range(G):
                for hh in range(cc // SB_HEAD_DIM):
                    ref[g, n0 // SB_HEAD_DIM + hh] = res[g * tt:(g + 1) * tt, hh * SB_HEAD_DIM:(hh + 1) * SB_HEAD_DIM]

    @pl.when(j == 0)
    def _():
        group(lambda n0, r: tok(q_ref, n0, r * (SB_HEAD_DIM ** -0.5)))

    @pl.when(j == 1)
    def _():
        def ep(n0, r):
            tok(k_ref, n0, r)
            heads(kf_ref, n0, r)
        group(ep)

    @pl.when(j == 2)
    def _():
        def ep(n0, r):
            tok(v_ref, n0, r)
            heads(vf_ref, n0, r)
        group(ep)

    @pl.when(j == 3)
    def _():
        def ep(n0, r):
            ca_ref[:, n0:n0 + cc] = r
        group(ep)

    @pl.when(j == 4)
    def _():
        group(lambda n0, r: tok(u_ref, n0, ca_ref[:, n0:n0 + cc] * _sigmoid(r)))

    @pl.when(j == 5)
    def _():
        group(lambda n0, r: tok(qm_ref, n0, r * (MEM_HEAD_DIM ** -0.5)))

    for jj, ref in ((6, ga_ref), (7, gb_ref), (8, gc_ref)):
        @pl.when(j == jj)
        def _():
            group(lambda n0, r: tok(ref, n0, _sigmoid(r)))


def _inproj(x, g, w_bf, *, G, tt, dmajor):
    b, t, _ = x.shape
    rows = G * tt
    nt = t // tt

    def staggered(j0, make):
        def imap(i, s, j):
            n = i * nt + s
            n = jnp.where(j >= j0, n, jnp.maximum(n - 1, 0))
            return make(n // nt, n % nt)
        return imap

    tok_block = (G, tt, D_MODEL)
    tok_spec = lambda j0: pl.BlockSpec(tok_block, staggered(j0, lambda i, s: (i, s, 0)))
    if dmajor:
        head_spec = lambda j0: pl.BlockSpec((G, D_MODEL, tt), staggered(j0, lambda i, s: (i, 0, s)))
        head_shape = jax.ShapeDtypeStruct((b, D_MODEL, t), F32)
    else:
        head_spec = lambda j0: pl.BlockSpec((G, SB_HEADS, tt, SB_HEAD_DIM), staggered(j0, lambda i, s: (i, 0, s, 0)))
        head_shape = jax.ShapeDtypeStruct((b, SB_HEADS, t, SB_HEAD_DIM), F32)
    tok_bf = jax.ShapeDtypeStruct((b, t, D_MODEL), BF16)
    tok_f32 = jax.ShapeDtypeStruct((b, t, D_MODEL), F32)
    kinds = (tok_spec, tok_spec, tok_spec, head_spec, head_spec, tok_spec, tok_spec, tok_spec, tok_spec, tok_spec)
    return pl.pallas_call(
        functools.partial(_inproj_kernel, G=G, tt=tt, dmajor=dmajor),
        grid=(b // G, nt, N_IN_GROUPS),
        in_specs=[
            pl.BlockSpec(tok_block, lambda i, s, j: (i, s, 0)),
            pl.BlockSpec((1, D_MODEL), lambda i, s, j: (0, 0)),
            pl.BlockSpec((D_MODEL, D_MODEL), lambda i, s, j: (0, j)),
        ],
        out_specs=[kind(j0) for kind, j0 in zip(kinds, INPROJ_OUT_GROUP)],
        out_shape=[tok_bf, tok_bf, tok_bf, head_shape, head_shape, tok_f32, tok_bf, tok_bf, tok_bf, tok_bf],
        scratch_shapes=[pltpu.VMEM((rows, D_MODEL), BF16), pltpu.VMEM((rows, D_MODEL), F32)],
        compiler_params=_params(("arbitrary", "arbitrary", "arbitrary")),
        name="inproj",
    )(x, g, w_bf)


def _neg_upper(n):
    j = lax.broadcasted_iota(jnp.int32, (n, n), 0)
    s = lax.broadcasted_iota(jnp.int32, (n, n), 1)
    return jnp.where(j > s, -1.0, 0.0).astype(BF16)


def _strict_lower(n):
    r = lax.broadcasted_iota(jnp.int32, (n, n), 0)
    s = lax.broadcasted_iota(jnp.int32, (n, n), 1)
    return s < r


def _sb_mask(z, mask):
    return z if mask is None else jnp.where(mask, z, SB_MASKED_SCORE)


def _sb_scores(z, nu, mask):
    z = _sb_mask(z, mask)
    nl = _softplus(z)
    log_beta = z - nl
    excl = _dot(nl.astype(BF16), nu)
    tot = excl[:, 0:1] - nl[:, 0:1]
    return log_beta, excl, tot


def _sb_weights(log_beta, excl, c):
    return jnp.exp(log_beta + excl + c).astype(BF16)


def _sb_prompt_kernel(q_ref, k_ref, v_ref, nu_ref, o_ref, acc_ref, c_ref, *, tq):
    qi = pl.program_id(2)
    nu = nu_ref[...]
    mask = _strict_lower(tq)
    q = q_ref[0]
    hsl = [slice(h * SB_HEAD_DIM, (h + 1) * SB_HEAD_DIM) for h in range(2)]

    def kv(kb):
        start = pl.multiple_of(kb * tq, tq)
        return k_ref[0, pl.ds(start, tq), :], v_ref[0, pl.ds(start, tq), :]

    def cmax():
        return jnp.maximum(jnp.max(c_ref[0]), jnp.max(c_ref[1]))

    @pl.when(qi == 0)
    def _():
        kd, vd = kv(qi)
        for h in range(2):
            lb, ex, tot = _sb_scores(_dot_nt(q[:, hsl[h]], kd[:, hsl[h]]), nu, mask)
            acc_ref[h] = _dot(_sb_weights(lb, ex, 0.0), vd[:, hsl[h]])
            c_ref[h] = tot

    @pl.when(qi > 0)
    def _():
        kd, vd = kv(qi)
        kp, vp = kv(qi - 1)
        for h in range(2):
            lb, ex, tot = _sb_scores(_dot_nt(q[:, hsl[h]], kd[:, hsl[h]]), nu, mask)
            lbp, exp_, totp = _sb_scores(_dot_nt(q[:, hsl[h]], kp[:, hsl[h]]), nu, None)
            o = _dot(_sb_weights(lb, ex, 0.0), vd[:, hsl[h]])
            o = o + _dot(_sb_weights(lbp, exp_, tot), vp[:, hsl[h]])
            acc_ref[h] = o
            c_ref[h] = tot + totp

    def cond(carry):
        kb, cm = carry
        return jnp.logical_and(kb >= 0, cm > SB_DEAD_LOG)

    def body(carry):
        kb, _ = carry
        kk, vv = kv(kb)
        for h in range(2):
            lb, ex, tot = _sb_scores(_dot_nt(q[:, hsl[h]], kk[:, hsl[h]]), nu, None)
            c = c_ref[h]
            acc_ref[h] += _dot(_sb_weights(lb, ex, c), vv[:, hsl[h]])
            c_ref[h] = c + tot
        return kb - 1, cmax()

    lax.while_loop(cond, body, (qi - 2, cmax()))
    o_ref[0] = jnp.concatenate([acc_ref[0], acc_ref[1]], axis=1).astype(BF16)


def _sb_near_kernel(q_ref, kd_ref, kp_ref, vd_ref, vp_ref, nu_ref, o_ref, flag_ref, *, tq, hb):
    qi = pl.program_id(2)
    nu = nu_ref[...]
    mask = _strict_lower(tq)
    q, kd, vd = q_ref[0], kd_ref[0], vd_ref[0]
    hs = [slice(h * SB_HEAD_DIM, (h + 1) * SB_HEAD_DIM) for h in range(hb)]

    @pl.when(qi == 0)
    def _():
        outs = []
        for h in range(hb):
            lb, ex, _ = _sb_scores(_dot_nt(q[:, hs[h]], kd[:, hs[h]]), nu, mask)
            outs.append(_dot(_sb_weights(lb, ex, 0.0), vd[:, hs[h]]))
        o_ref[0] = jnp.concatenate(outs, axis=1).astype(BF16)
        flag_ref[...] = jnp.full(flag_ref.shape, 2.0 * SB_DEAD_LOG, F32)

    @pl.when(qi > 0)
    def _():
        kp, vp = kp_ref[0], vp_ref[0]
        outs = []
        cm = None
        for h in range(hb):
            lb, ex, tot = _sb_scores(_dot_nt(q[:, hs[h]], kd[:, hs[h]]), nu, mask)
            lbp, exp_, totp = _sb_scores(_dot_nt(q[:, hs[h]], kp[:, hs[h]]), nu, None)
            o = _dot(_sb_weights(lb, ex, 0.0), vd[:, hs[h]])
            outs.append(o + _dot(_sb_weights(lbp, exp_, tot), vp[:, hs[h]]))
            m = jnp.max(tot + totp)
            cm = m if cm is None else jnp.maximum(cm, m)
        o_ref[0] = jnp.concatenate(outs, axis=1).astype(BF16)
        flag_ref[...] = jnp.full(flag_ref.shape, jnp.where(qi > 1, cm, 2.0 * SB_DEAD_LOG), F32)


def _sb_prompt(q, k, v, *, tq):
    b, t, _ = q.shape
    hb = SB_PROMPT_HEADS
    ng = SB_HEADS // hb
    nq = t // tq
    lanes = hb * SB_HEAD_DIM
    cur = pl.BlockSpec((1, tq, lanes), lambda i, h, s: (i, s, h))
    prev = pl.BlockSpec((1, tq, lanes), lambda i, h, s: (i, jnp.maximum(s - 1, 0), h))
    o, flag = pl.pallas_call(
        functools.partial(_sb_near_kernel, tq=tq, hb=hb),
        grid=(b, ng, nq),
        in_specs=[cur, cur, prev, cur, prev, pl.BlockSpec((tq, tq), lambda i, h, s: (0, 0))],
        out_specs=[cur, pl.BlockSpec((1, 1, 1, SUBLANES, LANES), lambda i, h, s: (i, h, s, 0, 0))],
        out_shape=[jax.ShapeDtypeStruct((b, t, D_MODEL), BF16),
                   jax.ShapeDtypeStruct((b, ng, nq, SUBLANES, LANES), F32)],
        compiler_params=_params(("arbitrary", "arbitrary", "arbitrary")),
        name="sb_near",
    )(q, k, k, v, v, _neg_upper(tq))
    return lax.cond(jnp.max(flag) > SB_DEAD_LOG, lambda: _sb_prompt_full(q, k, v, tq=tq), lambda: o)


def _sb_prompt_full(q, k, v, *, tq):
    b, t, _ = q.shape
    hp = SB_HEADS // 2
    lanes = 2 * SB_HEAD_DIM
    nu = _neg_upper(tq)
    return pl.pallas_call(
        functools.partial(_sb_prompt_kernel, tq=tq),
        grid=(b, hp, t // tq),
        in_specs=[
            pl.BlockSpec((1, tq, lanes), lambda i, h, s: (i, s, h)),
            pl.BlockSpec((1, t, lanes), lambda i, h, s: (i, 0, h)),
            pl.BlockSpec((1, t, lanes), lambda i, h, s: (i, 0, h)),
            pl.BlockSpec((tq, tq), lambda i, h, s: (0, 0)),
        ],
        out_specs=pl.BlockSpec((1, tq, lanes), lambda i, h, s: (i, s, h)),
        out_shape=jax.ShapeDtypeStruct((b, t, D_MODEL), BF16),
        scratch_shapes=[pltpu.VMEM((2, tq, SB_HEAD_DIM), F32), pltpu.VMEM((2, tq, 1), F32)],
        compiler_params=_params(("arbitrary", "arbitrary", "arbitrary")),
        name="sb_prompt",
    )(q, k, v, nu)


def _sb_sample_kernel(q_ref, kn_ref, vn_ref, kc_ref, vc_ref, nun_ref, nup_ref, acc_ref, c_ref, flag_ref,
                      *, t, hb, nblk, tk):
    mask = _strict_lower(t)
    q, kn, vn = q_ref[0], kn_ref[0], vn_ref[0]
    hs = [slice(h * SB_HEAD_DIM, (h + 1) * SB_HEAD_DIM) for h in range(hb)]
    ks = [slice(kb * tk, (kb + 1) * tk) for kb in range(nblk - 1, -1, -1)]

    lbs, nls = [], []
    for h in range(hb):
        qh = q[:, hs[h]]
        zs = [_sb_mask(_dot_nt(qh, kn[:, hs[h]]), mask)]
        zs += [_dot(qh, kc_ref[0, h, :, s].astype(BF16)) for s in ks]
        nl = [_softplus(z) for z in zs]
        lbs.append([z - n for z, n in zip(zs, nl)])
        nls.append(nl)

    excl = []
    for sg in range(nblk + 1):
        stack = jnp.concatenate([nls[h][sg] for h in range(hb)], axis=0).astype(BF16)
        excl.append(_dot(stack, nun_ref[...] if sg == 0 else nup_ref[...]))

    cm = None
    for h in range(hb):
        rs = slice(h * t, (h + 1) * t)
        ex = excl[0][rs]
        o = _dot(_sb_weights(lbs[h][0], ex, 0.0), vn[:, hs[h]])
        c = ex[:, 0:1] - nls[h][0][:, 0:1]
        for i, s in enumerate(ks):
            ex = excl[i + 1][rs]
            o = o + _dot_nt(_sb_weights(lbs[h][i + 1], ex, c), vc_ref[0, h, :, s].astype(BF16))
            c = c + (ex[:, 0:1] - nls[h][i + 1][:, 0:1])
        acc_ref[0, :, hs[h]] = o
        c_ref[0, h] = jnp.broadcast_to(c, (t, LANES))
        m = jnp.max(c)
        cm = m if cm is None else jnp.maximum(cm, m)
    flag_ref[...] = jnp.full(flag_ref.shape, cm, F32)


def _sb_tail_kernel(q_ref, kc_ref, vc_ref, nu_ref, acc_in_ref, c_in_ref, acc_ref, acc_scr, c_scr):
    s = pl.program_id(2)

    @pl.when(s == 0)
    def _():
        for h in range(2):
            acc_scr[h] = acc_in_ref[0][:, h * SB_HEAD_DIM:(h + 1) * SB_HEAD_DIM]
            c_scr[h] = c_in_ref[0, h][:, 0:1]

    @pl.when(jnp.maximum(jnp.max(c_scr[0]), jnp.max(c_scr[1])) > SB_DEAD_LOG)
    def _():
        for h in range(2):
            qh = q_ref[0][:, h * SB_HEAD_DIM:(h + 1) * SB_HEAD_DIM]
            lb, ex, tot = _sb_scores(_dot(qh, kc_ref[0, h].astype(BF16)), nu_ref[...], None)
            c = c_scr[h]
            acc_scr[h] += _dot_nt(_sb_weights(lb, ex, c), vc_ref[0, h].astype(BF16))
            c_scr[h] = c + tot

    @pl.when(s == pl.num_programs(2) - 1)
    def _():
        acc_ref[0] = jnp.concatenate([acc_scr[0], acc_scr[1]], axis=1)


def _sb_sample(q, kn, vn, kct, vct):
    b, t, _ = q.shape
    past = kct.shape[3]
    tk = SB_BLOCK
    near = min(SB_SAMPLE_NEAR, past)
    hb = SB_SAMPLE_HEADS
    ng = SB_HEADS // hb
    new_spec = pl.BlockSpec((1, t, hb * SB_HEAD_DIM), lambda i, h: (i, 0, h))
    cache_spec = pl.BlockSpec((1, hb, SB_HEAD_DIM, near), lambda i, h: (i, h, 0, past // near - 1))
    nun, nup = _neg_upper(t), _neg_upper(tk)
    acc, c, flag = pl.pallas_call(
        functools.partial(_sb_sample_kernel, t=t, hb=hb, nblk=near // tk, tk=tk),
        grid=(b, ng),
        in_specs=[new_spec, new_spec, new_spec, cache_spec, cache_spec,
                  pl.BlockSpec((t, t), lambda i, h: (0, 0)),
                  pl.BlockSpec((tk, tk), lambda i, h: (0, 0))],
        out_specs=[new_spec,
                   pl.BlockSpec((1, hb, t, LANES), lambda i, h: (i, h, 0, 0)),
                   pl.BlockSpec((1, 1, SUBLANES, LANES), lambda i, h: (i, h, 0, 0))],
        out_shape=[jax.ShapeDtypeStruct((b, t, D_MODEL), F32),
                   jax.ShapeDtypeStruct((b, SB_HEADS, t, LANES), F32),
                   jax.ShapeDtypeStruct((b, ng, SUBLANES, LANES), F32)],
        compiler_params=_params(("arbitrary", "arbitrary")),
        name="sb_sample",
    )(q, kn, vn, kct, vct, nun, nup)
    nfar = (past - near) // tk
    if nfar == 0:
        return acc

    def tail():
        pair = pl.BlockSpec((1, t, 2 * SB_HEAD_DIM), lambda i, h, s: (i, 0, h))
        far = pl.BlockSpec((1, 2, SB_HEAD_DIM, tk), lambda i, h, s: (i, h, 0, nfar - 1 - s))
        return pl.pallas_call(
            _sb_tail_kernel,
            grid=(b, SB_HEADS // 2, nfar),
            in_specs=[pair, far, far, pl.BlockSpec((tk, tk), lambda i, h, s: (0, 0)), pair,
                      pl.BlockSpec((1, 2, t, LANES), lambda i, h, s: (i, h, 0, 0))],
            out_specs=pair,
            out_shape=jax.ShapeDtypeStruct((b, t, D_MODEL), F32),
            scratch_shapes=[pltpu.VMEM((2, t, SB_HEAD_DIM), F32), pltpu.VMEM((2, t, 1), F32)],
            compiler_params=_params(("arbitrary", "arbitrary", "arbitrary")),
            name="sb_tail",
        )(q, kct, vct, nup, acc, c)

    return lax.cond(jnp.max(flag) > SB_DEAD_LOG, tail, lambda: acc)


CONV_ROWS = 64
CONV_LANES = 256


def _mix_kernel(x_ref, osb_ref, u_ref, left_ref, qm_ref, ga_ref, gb_ref, gc_ref, mk_ref, mv_ref,
                wsb_ref, wcv_ref, wmo_ref, wout_ref, dww_ref, dwb_ref, lng_ref, lnb_ref, gpost_ref,
                y_ref, uf_ref, cc_ref, om_ref, *, G, tt):
    s = pl.program_id(1)
    rows = G * tt
    H0 = CONV_HALO
    off = H0 - (CONV_WIDTH - 1)

    @pl.when(s == 0)
    def _():
        uf_ref[:, 0:H0, :] = left_ref[...]

    @pl.when(s > 0)
    def _():
        uf_ref[:, 0:H0, :] = uf_ref[:, tt:tt + H0, :]

    uf_ref[:, H0:H0 + tt, :] = u_ref[...]

    rc = min(CONV_ROWS, tt)
    win = rc + H0
    for g in range(G):
        for r0 in range(0, tt, rc):
            for c0 in range(0, D_MODEL, CONV_LANES):
                cs = slice(c0, c0 + CONV_LANES)
                window = uf_ref[g, r0:r0 + win, cs]
                acc = jnp.broadcast_to(dwb_ref[:, cs], (rc, CONV_LANES))
                for r in range(SUBLANES):
                    shifted = window if r == 0 else pltpu.roll(window, win - r, axis=0)
                    for m in range(H0 // SUBLANES + 1):
                        j = SUBLANES * m + r - off
                        if 0 <= j < CONV_WIDTH:
                            acc = acc + shifted[SUBLANES * m:SUBLANES * m + rc] * dww_ref[j:j + 1, cs]
                cc_ref[g * tt + r0:g * tt + r0 + rc, cs] = acc

    cc = cc_ref[...]
    mu = jnp.mean(cc, axis=-1, keepdims=True)
    d = cc - mu
    var = jnp.mean(d * d, axis=-1, keepdims=True)
    yn = d * lax.rsqrt(var + NORM_EPS) * lng_ref[...] + lnb_ref[...]
    y_conv = _dot((yn * _sigmoid(yn)).astype(BF16), wcv_ref[...])

    y_sb = _dot(osb_ref[...].reshape(rows, D_MODEL).astype(BF16), wsb_ref[...])

    for g in range(G):
        for h in range(MEM_HEADS):
            cs = slice(h * MEM_HEAD_DIM, (h + 1) * MEM_HEAD_DIM)
            sc = _dot_nt(qm_ref[g, :, cs], mk_ref[g, h].astype(BF16))
            p = jnp.exp(sc - jnp.max(sc, axis=-1, keepdims=True))
            den = jnp.sum(p, axis=-1, keepdims=True)
            oh = _dot(p.astype(BF16), mv_ref[g, h].astype(BF16)) * (1.0 / den)
            om_ref[g * tt:(g + 1) * tt, cs] = oh.astype(BF16)
    y_mem = _dot(om_ref[...], wmo_ref[...])

    ga = ga_ref[...].reshape(rows, D_MODEL).astype(F32)
    gb = gb_ref[...].reshape(rows, D_MODEL).astype(F32)
    gc = gc_ref[...].reshape(rows, D_MODEL).astype(F32)
    merged = ga * y_sb + gb * y_conv + gc * y_mem
    proj = _dot(merged.astype(BF16), wout_ref[...])
    y = x_ref[...].reshape(rows, D_MODEL) + _rms(proj, gpost_ref[...])
    y_ref[...] = y.reshape(G, tt, D_MODEL)


def _mix(x, osb, u, left, qm, ga, gb, gc, mk, mv, wsb, wcv, wmo, wout, dww, dwb, lng, lnb, gpost, *, G, tt):
    b, t, _ = x.shape
    rows = G * tt
    tok = pl.BlockSpec((G, tt, D_MODEL), lambda i, s: (i, s, 0))
    mem = pl.BlockSpec((G, MEM_HEADS, MEM_TOKENS, MEM_HEAD_DIM), lambda i, s: (i, 0, 0, 0))
    wsq = pl.BlockSpec((D_MODEL, D_MODEL), lambda i, s: (0, 0))
    vec = pl.BlockSpec((1, D_MODEL), lambda i, s: (0, 0))
    return pl.pallas_call(
        functools.partial(_mix_kernel, G=G, tt=tt),
        grid=(b // G, t // tt),
        in_specs=[tok, tok, tok,
                  pl.BlockSpec((G, CONV_HALO, D_MODEL), lambda i, s: (i, 0, 0)),
                  tok, tok, tok, tok, mem, mem, wsq, wsq, wsq, wsq,
                  pl.BlockSpec((CONV_WIDTH, D_MODEL), lambda i, s: (0, 0)),
                  vec, vec, vec, vec],
        out_specs=tok,
        out_shape=jax.ShapeDtypeStruct((b, t, D_MODEL), F32),
        scratch_shapes=[pltpu.VMEM((G, CONV_HALO + tt, D_MODEL), F32),
                        pltpu.VMEM((rows, D_MODEL), F32),
                        pltpu.VMEM((rows, D_MODEL), BF16)],
        compiler_params=_params(("arbitrary", "arbitrary")),
        name="mix",
    )(x, osb, u, left, qm, ga, gb, gc, mk, mv, wsb, wcv, wmo, wout, dww, dwb, lng, lnb, gpost)


FFN_ROWS = 256


def _gelu_tanh(x):
    return 0.5 * x * (1.0 + jnp.tanh(0.7978845608028654 * (x + 0.044715 * (x * x * x))))


def _ffn_kernel(x_ref, lg_ref, lv_ref, gpre_ref, gpost_ref, wg_ref, wv_ref, dwg_ref, dwv_ref, wd_ref,
                y_ref, st_ref, hb_ref, acc_ref, upf_ref, carry_ref, *, G, tt, nf):
    s = pl.program_id(1)
    f = pl.program_id(2)
    rows = G * tt
    S = SUBLANES
    lefts = (lg_ref, lv_ref)
    ws = (wg_ref, wv_ref)
    dws = (dwg_ref, dwv_ref)

    @pl.when(f == 0)
    def _():
        hb_ref[...] = _rms(x_ref[...].reshape(rows, D_MODEL), gpre_ref[...]).astype(BF16)
        acc_ref[...] = jnp.zeros_like(acc_ref)

    for half in range(2):
        @pl.when(s == 0)
        def _():
            upf_ref[half, :, 0:S, :] = lefts[half][...]

        @pl.when(s > 0)
        def _():
            upf_ref[half, :, 0:S, :] = carry_ref[f, half]

    rc = min(FFN_ROWS, rows)
    gpc = max(rc // tt, 1)
    tc = min(rc, tt)
    for r0 in range(0, rows, rc):
        g0, t0 = r0 // tt, r0 % tt
        hb = hb_ref[r0:r0 + rc, :]
        conv = []
        for half in range(2):
            up = _dot(hb, ws[half][...])
            upf_ref[half, g0:g0 + gpc, S + t0:S + t0 + tc, :] = up.reshape(gpc, tc, -1)
            dw = dws[half]
            c = (up.reshape(gpc, tc, -1) * dw[2:3, :]
                 + upf_ref[half, g0:g0 + gpc, S + t0 - 1:S + t0 - 1 + tc, :] * dw[1:2, :]
                 + upf_ref[half, g0:g0 + gpc, S + t0 - 2:S + t0 - 2 + tc, :] * dw[0:1, :])
            conv.append(c.reshape(rc, -1))
        act = (_gelu_tanh(conv[0]) * conv[1]).astype(BF16)
        acc_ref[r0:r0 + rc, :] += _dot(act, wd_ref[...])

    for half in range(2):
        last = upf_ref[half, :, tt:tt + S, :]
        carry_ref[f, half] = last
        st_ref[half, f] = last

    @pl.when(f == nf - 1)
    def _():
        y = x_ref[...].reshape(rows, D_MODEL) + _rms(acc_ref[...], gpost_ref[...])
        y_ref[...] = y.reshape(G, tt, D_MODEL)


def _ffn(x, left, gpre, gpost, wup, dw, wdown, *, G, tt, nf):
    b, t, _ = x.shape
    rows = G * tt
    tf = FFN_DIM // nf
    S = SUBLANES
    tok = pl.BlockSpec((G, tt, D_MODEL), lambda i, s, f: (i, s, 0))
    vec = pl.BlockSpec((1, D_MODEL), lambda i, s, f: (0, 0))
    y, st = pl.pallas_call(
        functools.partial(_ffn_kernel, G=G, tt=tt, nf=nf),
        grid=(b // G, t // tt, nf),
        in_specs=[tok,
                  pl.BlockSpec((G, S, tf), lambda i, s, f: (i, 0, f)),
                  pl.BlockSpec((G, S, tf), lambda i, s, f: (i, 0, nf + f)),
                  vec, vec,
                  pl.BlockSpec((D_MODEL, tf), lambda i, s, f: (0, f)),
                  pl.BlockSpec((D_MODEL, tf), lambda i, s, f: (0, nf + f)),
                  pl.BlockSpec((FFN_CONV_WIDTH, tf), lambda i, s, f: (0, f)),
                  pl.BlockSpec((FFN_CONV_WIDTH, tf), lambda i, s, f: (0, nf + f)),
                  pl.BlockSpec((tf, D_MODEL), lambda i, s, f: (f, 0))],
        out_specs=[tok, pl.BlockSpec((2, nf, G, S, tf), lambda i, s, f: (0, 0, i, 0, 0))],
        out_shape=[jax.ShapeDtypeStruct((b, t, D_MODEL), F32),
                   jax.ShapeDtypeStruct((2, nf, b, S, tf), F32)],
        scratch_shapes=[pltpu.VMEM((rows, D_MODEL), BF16),
                        pltpu.VMEM((rows, D_MODEL), F32),
                        pltpu.VMEM((2, G, S + tt, tf), F32),
                        pltpu.VMEM((nf, 2, G, S, tf), F32)],
        compiler_params=_params(("arbitrary", "arbitrary", "arbitrary")),
        name="ffn",
    )(x, left, left, gpre, gpost, wup, wup, dw, dw, wdown)
    keep = FFN_CONV_WIDTH - 1
    state = st[:, :, :, S - keep:, :].transpose(2, 3, 0, 1, 4).reshape(b, keep, 2 * FFN_DIM)
    return y, state


def _tiles(b, t):
    if t >= 512:
        return {"inproj": (1, 512), "mix": (1, 256), "ffn": (1, 512)}
    return {"inproj": (min(b, 512 // t), t), "mix": (min(b, 256 // t), t), "ffn": (min(b, 512 // t), t)}


def _layer(x, mk, mv, sb_cache, conv_left, ffn_left, w):
    b, t, _ = x.shape
    tiles = _tiles(b, t)
    prompt = sb_cache is None
    G, tt = tiles["inproj"]
    q, k, v, kf, vf, u, qm, ga, gb, gc = _inproj(x, w["g_mix_pre"], w["w_in"], G=G, tt=tt, dmajor=prompt)
    if prompt:
        osb = _sb_prompt(q, k, v, tq=SB_BLOCK)
        kf = kf.reshape(b, SB_HEADS, SB_HEAD_DIM, t).swapaxes(2, 3)
        vf = vf.reshape(b, SB_HEADS, SB_HEAD_DIM, t).swapaxes(2, 3)
    else:
        osb = _sb_sample(q, k, v, sb_cache[0].swapaxes(2, 3), sb_cache[1].swapaxes(2, 3))

    keep = CONV_WIDTH - 1
    left = jnp.pad(conv_left, ((0, 0), (CONV_HALO - keep, 0), (0, 0)))
    G, tt = tiles["mix"]
    x1 = _mix(x, osb, u, left, qm, ga, gb, gc, mk, mv, w["w_sb_o"], w["w_conv_o"], w["w_mem_o"], w["w_out"],
              w["conv_dw_w"], w["conv_dw_b"], w["conv_ln_g"], w["conv_ln_b"], w["g_mix_post"], G=G, tt=tt)
    conv_state = jnp.concatenate([conv_left, u], axis=1)[:, -keep:] if t < keep else u[:, t - keep:]

    fkeep = FFN_CONV_WIDTH - 1
    fleft = jnp.pad(ffn_left, ((0, 0), (SUBLANES - fkeep, 0), (0, 0)))
    G, tt = tiles["ffn"]
    y, ffn_state = _ffn(x1, fleft, w["g_ffn_pre"], w["g_ffn_post"], w["w_ffn_up"], w["ffn_dw_w"],
                        w["w_ffn_down"], G=G, tt=tt, nf=2)
    return y, kf, vf, conv_state, ffn_state


def kernel(x_prompt, x_sample, mem_prompt, cache_sb_k, cache_sb_v, state_conv, state_ffn_conv, cache_mem_k, cache_mem_v, g_mem, w_mem_kv, g_mix_pre, g_mix_post, w_in, w_sb_o, conv_dw_w, conv_dw_b, conv_ln_g, conv_ln_b, w_conv_o, w_mem_o, w_out, g_ffn_pre, g_ffn_post, w_ffn_up, ffn_dw_w, w_ffn_down):
    depth = w_in.shape[0]
    bp = x_prompt.shape[0]
    yp, ys = x_prompt, x_sample
    outs = [[] for _ in range(10)]
    for l in range(depth):
        vec = lambda a: a[l].reshape(1, -1)
        w = {
            "g_mix_pre": vec(g_mix_pre), "g_mix_post": vec(g_mix_post),
            "w_in": w_in[l].astype(BF16), "w_sb_o": w_sb_o[l].astype(BF16),
            "conv_dw_w": conv_dw_w[l], "conv_dw_b": vec(conv_dw_b),
            "conv_ln_g": vec(conv_ln_g), "conv_ln_b": vec(conv_ln_b),
            "w_conv_o": w_conv_o[l].astype(BF16), "w_mem_o": w_mem_o[l].astype(BF16),
            "w_out": w_out[l].astype(BF16),
            "g_ffn_pre": vec(g_ffn_pre), "g_ffn_post": vec(g_ffn_post),
            "w_ffn_up": w_ffn_up[l].astype(BF16), "ffn_dw_w": ffn_dw_w[l],
            "w_ffn_down": w_ffn_down[l].astype(BF16),
        }
        mkv, mkv_bf = _memkv(mem_prompt, vec(g_mem), w_mem_kv[l].astype(BF16))
        yp, kp, vp, cp, fp = _layer(
            yp, mkv_bf[0], mkv_bf[1], None,
            jnp.zeros((bp, CONV_WIDTH - 1, D_MODEL), F32),
            jnp.zeros((bp, FFN_CONV_WIDTH - 1, 2 * FFN_DIM), F32), w)
        ys, ks, vs, cs, fs = _layer(
            ys, cache_mem_k[l], cache_mem_v[l], (cache_sb_k[l], cache_sb_v[l]),
            state_conv[l], state_ffn_conv[l], w)
        for lst, val in zip(outs, (kp, vp, ks, vs, cp, cs, fp, fs, mkv[0], mkv[1])):
            lst.append(val)
    return (yp, ys) + tuple(jnp.stack(o) for o in outs)
```

```python
import functools

import jax
import jax.numpy as jnp
from jax import lax
from jax.experimental import pallas as pl
from jax.experimental.pallas import tpu as pltpu

F32 = jnp.float32
BF16 = jnp.bfloat16

D_MODEL = 1024
SB_HEADS = 16
SB_HEAD_DIM = 64
MEM_HEADS = 4
MEM_HEAD_DIM = 256
MEM_TOKENS = 256
CONV_WIDTH = 31
FFN_DIM = 2816
FFN_CONV_WIDTH = 3
NORM_EPS = 1e-6

SUBLANES = 8
LANES = 128
CONV_HALO = 32
VMEM_LIMIT_BYTES = 56 * 1024 * 1024

SB_DEAD_LOG = -110.0
SB_MASKED_SCORE = -1e4
SB_BLOCK = 256
SB_PROMPT_HEADS = 8
SB_SAMPLE_HEADS = 8
SB_SAMPLE_NEAR = 512


def _rms(x, g):
    return x * lax.rsqrt(jnp.mean(x * x, axis=-1, keepdims=True) + NORM_EPS) * g


def _sigmoid(x):
    return 0.5 * jnp.tanh(0.5 * x) + 0.5


def _softplus(z):
    sign = jnp.uint32(0x80000000)
    neg_abs = lax.bitcast_convert_type(lax.bitcast_convert_type(z, jnp.uint32) | sign, F32)
    return jnp.maximum(z, 0.0) + jnp.log(1.0 + jnp.exp(neg_abs))


def _dot(a, b):
    return jnp.dot(a, b, preferred_element_type=F32)


def _dot_nt(a, b):
    return lax.dot_general(a, b, (((1,), (1,)), ((), ())), preferred_element_type=F32)


def _params(sem):
    return pltpu.CompilerParams(dimension_semantics=sem, vmem_limit_bytes=VMEM_LIMIT_BYTES)


def _resident(block, index_map):
    return pl.BlockSpec(block, index_map, pipeline_mode=pl.Buffered(1))


def _memkv_kernel(mem_ref, g_ref, w_ref, o_ref, ob_ref, hb_ref):
    @pl.when(pl.program_id(1) == 0)
    def _():
        hb_ref[...] = _rms(mem_ref[0], g_ref[...]).astype(BF16)

    res = _dot(hb_ref[...], w_ref[...])
    o_ref[0, 0, 0] = res
    ob_ref[0, 0, 0] = res.astype(BF16)


def _memkv(mem, g, w_bf):
    b = mem.shape[0]
    nj = 2 * MEM_HEADS
    out_map = lambda i, j: (j // MEM_HEADS, i, j % MEM_HEADS, 0, 0)
    shape = (2, b, MEM_HEADS, MEM_TOKENS, MEM_HEAD_DIM)
    return pl.pallas_call(
        _memkv_kernel,
        grid=(b, nj),
        in_specs=[
            pl.BlockSpec((1, MEM_TOKENS, D_MODEL), lambda i, j: (i, 0, 0)),
            pl.BlockSpec((1, D_MODEL), lambda i, j: (0, 0)),
            pl.BlockSpec((D_MODEL, MEM_HEAD_DIM), lambda i, j: (0, j)),
        ],
        out_specs=[
            pl.BlockSpec((1, 1, 1, MEM_TOKENS, MEM_HEAD_DIM), out_map),
            pl.BlockSpec((1, 1, 1, MEM_TOKENS, MEM_HEAD_DIM), out_map),
        ],
        out_shape=[jax.ShapeDtypeStruct(shape, F32), jax.ShapeDtypeStruct(shape, BF16)],
        scratch_shapes=[pltpu.VMEM((MEM_TOKENS, D_MODEL), BF16)],
        compiler_params=_params(("arbitrary", "arbitrary")),
        name="memkv",
    )(mem, g, w_bf)


QKV_COLS = 512


def _qkv_kernel(x_ref, g_ref, w_ref, q_ref, k_ref, v_ref, kf_ref, vf_ref, hb_ref, *, G, tt, dmajor):
    rows = G * tt
    cc = QKV_COLS
    hb_ref[...] = _rms(x_ref[...].reshape(rows, D_MODEL), g_ref[...]).astype(BF16)

    def tok(ref, n0, val):
        ref[:, :, n0:n0 + cc] = val.reshape(G, tt, cc).astype(ref.dtype)

    def heads(ref, n0, res):
        if dmajor:
            ref[0, n0:n0 + cc, :] = res.T
        else:
            for g in range(G):
                for hh in range(cc // SB_HEAD_DIM):
                    ref[g, n0 // SB_HEAD_DIM + hh] = res[g * tt:(g + 1) * tt, hh * SB_HEAD_DIM:(hh + 1) * SB_HEAD_DIM]

    for grp, (tref, href) in enumerate(((q_ref, None), (k_ref, kf_ref), (v_ref, vf_ref))):
        for n0 in range(0, D_MODEL, cc):
            res = _dot(hb_ref[...], w_ref[:, grp * D_MODEL + n0:grp * D_MODEL + n0 + cc])
            tok(tref, n0, res * (SB_HEAD_DIM ** -0.5) if href is None else res)
            if href is not None:
                heads(href, n0, res)


def _qkv(x, g, w_bf, *, G, tt, dmajor):
    b, t, _ = x.shape
    rows = G * tt
    tok_spec = pl.BlockSpec((G, tt, D_MODEL), lambda i, s: (i, s, 0))
    if dmajor:
        head_spec = pl.BlockSpec((G, D_MODEL, tt), lambda i, s: (i, 0, s))
        head_shape = jax.ShapeDtypeStruct((b, D_MODEL, t), F32)
    else:
        head_spec = pl.BlockSpec((G, SB_HEADS, tt, SB_HEAD_DIM), lambda i, s: (i, 0, s, 0))
        head_shape = jax.ShapeDtypeStruct((b, SB_HEADS, t, SB_HEAD_DIM), F32)
    tok_bf = jax.ShapeDtypeStruct((b, t, D_MODEL), BF16)
    return pl.pallas_call(
        functools.partial(_qkv_kernel, G=G, tt=tt, dmajor=dmajor),
        grid=(b // G, t // tt),
        in_specs=[tok_spec,
                  pl.BlockSpec((1, D_MODEL), lambda i, s: (0, 0)),
                  _resident((D_MODEL, 3 * D_MODEL), lambda i, s: (0, 0))],
        out_specs=[tok_spec, tok_spec, tok_spec, head_spec, head_spec],
        out_shape=[tok_bf, tok_bf, tok_bf, head_shape, head_shape],
        scratch_shapes=[pltpu.VMEM((rows, D_MODEL), BF16)],
        compiler_params=_params(("arbitrary", "arbitrary")),
        name="qkv",
    )(x, g, w_bf)


def _neg_upper(n):
    j = lax.broadcasted_iota(jnp.int32, (n, n), 0)
    s = lax.broadcasted_iota(jnp.int32, (n, n), 1)
    return jnp.where(j > s, -1.0, 0.0).astype(BF16)


def _strict_lower(n):
    r = lax.broadcasted_iota(jnp.int32, (n, n), 0)
    s = lax.broadcasted_iota(jnp.int32, (n, n), 1)
    return s < r


def _sb_mask(z, mask):
    return z if mask is None else jnp.where(mask, z, SB_MASKED_SCORE)


def _sb_scores(z, nu, mask):
    z = _sb_mask(z, mask)
    nl = _softplus(z)
    log_beta = z - nl
    excl = _dot(nl.astype(BF16), nu)
    tot = excl[:, 0:1] - nl[:, 0:1]
    return log_beta, excl, tot


def _sb_weights(log_beta, excl, c):
    return jnp.exp(log_beta + excl + c).astype(BF16)


def _sb_prompt_kernel(q_ref, k_ref, v_ref, nu_ref, o_ref, acc_ref, c_ref, *, tq):
    qi = pl.program_id(2)
    nu = nu_ref[...]
    mask = _strict_lower(tq)
    q = q_ref[0]
    hsl = [slice(h * SB_HEAD_DIM, (h + 1) * SB_HEAD_DIM) for h in range(2)]

    def kv(kb):
        start = pl.multiple_of(kb * tq, tq)
        return k_ref[0, pl.ds(start, tq), :], v_ref[0, pl.ds(start, tq), :]

    def cmax():
        return jnp.maximum(jnp.max(c_ref[0]), jnp.max(c_ref[1]))

    @pl.when(qi == 0)
    def _():
        kd, vd = kv(qi)
        for h in range(2):
            lb, ex, tot = _sb_scores(_dot_nt(q[:, hsl[h]], kd[:, hsl[h]]), nu, mask)
            acc_ref[h] = _dot(_sb_weights(lb, ex, 0.0), vd[:, hsl[h]])
            c_ref[h] = tot

    @pl.when(qi > 0)
    def _():
        kd, vd = kv(qi)
        kp, vp = kv(qi - 1)
        for h in range(2):
            lb, ex, tot = _sb_scores(_dot_nt(q[:, hsl[h]], kd[:, hsl[h]]), nu, mask)
            lbp, exp_, totp = _sb_scores(_dot_nt(q[:, hsl[h]], kp[:, hsl[h]]), nu, None)
            o = _dot(_sb_weights(lb, ex, 0.0), vd[:, hsl[h]])
            o = o + _dot(_sb_weights(lbp, exp_, tot), vp[:, hsl[h]])
            acc_ref[h] = o
            c_ref[h] = tot + totp

    def cond(carry):
        kb, cm = carry
        return jnp.logical_and(kb >= 0, cm > SB_DEAD_LOG)

    def body(carry):
        kb, _ = carry
        kk, vv = kv(kb)
        for h in range(2):
            lb, ex, tot = _sb_scores(_dot_nt(q[:, hsl[h]], kk[:, hsl[h]]), nu, None)
            c = c_ref[h]
            acc_ref[h] += _dot(_sb_weights(lb, ex, c), vv[:, hsl[h]])
            c_ref[h] = c + tot
        return kb - 1, cmax()

    lax.while_loop(cond, body, (qi - 2, cmax()))
    o_ref[0] = jnp.concatenate([acc_ref[0], acc_ref[1]], axis=1).astype(BF16)


def _sb_near_kernel(q_ref, kd_ref, kp_ref, vd_ref, vp_ref, nu_ref, o_ref, flag_ref, *, tq, hb):
    qi = pl.program_id(2)
    nu = nu_ref[...]
    mask = _strict_lower(tq)
    q, kd, vd = q_ref[0], kd_ref[0], vd_ref[0]
    hs = [slice(h * SB_HEAD_DIM, (h + 1) * SB_HEAD_DIM) for h in range(hb)]

    @pl.when(qi == 0)
    def _():
        outs = []
        for h in range(hb):
            lb, ex, _ = _sb_scores(_dot_nt(q[:, hs[h]], kd[:, hs[h]]), nu, mask)
            outs.append(_dot(_sb_weights(lb, ex, 0.0), vd[:, hs[h]]))
        o_ref[0] = jnp.concatenate(outs, axis=1).astype(BF16)
        flag_ref[...] = jnp.full(flag_ref.shape, 2.0 * SB_DEAD_LOG, F32)

    @pl.when(qi > 0)
    def _():
        kp, vp = kp_ref[0], vp_ref[0]
        outs = []
        cm = None
        for h in range(hb):
            lb, ex, tot = _sb_scores(_dot_nt(q[:, hs[h]], kd[:, hs[h]]), nu, mask)
            lbp, exp_, totp = _sb_scores(_dot_nt(q[:, hs[h]], kp[:, hs[h]]), nu, None)
            o = _dot(_sb_weights(lb, ex, 0.0), vd[:, hs[h]])
            outs.append(o + _dot(_sb_weights(lbp, exp_, tot), vp[:, hs[h]]))
            m = jnp.max(tot + totp)
            cm = m if cm is None else jnp.maximum(cm, m)
        o_ref[0] = jnp.concatenate(outs, axis=1).astype(BF16)
        flag_ref[...] = jnp.full(flag_ref.shape, jnp.where(qi > 1, cm, 2.0 * SB_DEAD_LOG), F32)


def _sb_prompt(q, k, v, *, tq):
    b, t, _ = q.shape
    hb = SB_PROMPT_HEADS
    ng = SB_HEADS // hb
    nq = t // tq
    lanes = hb * SB_HEAD_DIM
    cur = pl.BlockSpec((1, tq, lanes), lambda i, h, s: (i, s, h))
    prev = pl.BlockSpec((1, tq, lanes), lambda i, h, s: (i, jnp.maximum(s - 1, 0), h))
    o, flag = pl.pallas_call(
        functools.partial(_sb_near_kernel, tq=tq, hb=hb),
        grid=(b, ng, nq),
        in_specs=[cur, cur, prev, cur, prev, pl.BlockSpec((tq, tq), lambda i, h, s: (0, 0))],
        out_specs=[cur, pl.BlockSpec((1, 1, 1, SUBLANES, LANES), lambda i, h, s: (i, h, s, 0, 0))],
        out_shape=[jax.ShapeDtypeStruct((b, t, D_MODEL), BF16),
                   jax.ShapeDtypeStruct((b, ng, nq, SUBLANES, LANES), F32)],
        compiler_params=_params(("arbitrary", "arbitrary", "arbitrary")),
        name="sb_near",
    )(q, k, k, v, v, _neg_upper(tq))
    return lax.cond(jnp.max(flag) > SB_DEAD_LOG, lambda: _sb_prompt_full(q, k, v, tq=tq), lambda: o)


def _sb_prompt_full(q, k, v, *, tq):
    b, t, _ = q.shape
    hp = SB_HEADS // 2
    lanes = 2 * SB_HEAD_DIM
    nu = _neg_upper(tq)
    return pl.pallas_call(
        functools.partial(_sb_prompt_kernel, tq=tq),
        grid=(b, hp, t // tq),
        in_specs=[
            pl.BlockSpec((1, tq, lanes), lambda i, h, s: (i, s, h)),
            pl.BlockSpec((1, t, lanes), lambda i, h, s: (i, 0, h)),
            pl.BlockSpec((1, t, lanes), lambda i, h, s: (i, 0, h)),
            pl.BlockSpec((tq, tq), lambda i, h, s: (0, 0)),
        ],
        out_specs=pl.BlockSpec((1, tq, lanes), lambda i, h, s: (i, s, h)),
        out_shape=jax.ShapeDtypeStruct((b, t, D_MODEL), BF16),
        scratch_shapes=[pltpu.VMEM((2, tq, SB_HEAD_DIM), F32), pltpu.VMEM((2, tq, 1), F32)],
        compiler_params=_params(("arbitrary", "arbitrary", "arbitrary")),
        name="sb_prompt",
    )(q, k, v, nu)


def _sb_sample_kernel(q_ref, kn_ref, vn_ref, kc_ref, vc_ref, nun_ref, nup_ref, acc_ref, c_ref, flag_ref,
                      *, t, hb, nblk, tk):
    mask = _strict_lower(t)
    q, kn, vn = q_ref[0], kn_ref[0], vn_ref[0]
    hs = [slice(h * SB_HEAD_DIM, (h + 1) * SB_HEAD_DIM) for h in range(hb)]
    ks = [slice(kb * tk, (kb + 1) * tk) for kb in range(nblk - 1, -1, -1)]

    lbs, nls = [], []
    for h in range(hb):
        qh = q[:, hs[h]]
        zs = [_sb_mask(_dot_nt(qh, kn[:, hs[h]]), mask)]
        zs += [_dot(qh, kc_ref[0, h, :, s].astype(BF16)) for s in ks]
        nl = [_softplus(z) for z in zs]
        lbs.append([z - n for z, n in zip(zs, nl)])
        nls.append(nl)

    excl = []
    for sg in range(nblk + 1):
        stack = jnp.concatenate([nls[h][sg] for h in range(hb)], axis=0).astype(BF16)
        excl.append(_dot(stack, nun_ref[...] if sg == 0 else nup_ref[...]))

    cm = None
    for h in range(hb):
        rs = slice(h * t, (h + 1) * t)
        ex = excl[0][rs]
        o = _dot(_sb_weights(lbs[h][0], ex, 0.0), vn[:, hs[h]])
        c = ex[:, 0:1] - nls[h][0][:, 0:1]
        for i, s in enumerate(ks):
            ex = excl[i + 1][rs]
            o = o + _dot_nt(_sb_weights(lbs[h][i + 1], ex, c), vc_ref[0, h, :, s].astype(BF16))
            c = c + (ex[:, 0:1] - nls[h][i + 1][:, 0:1])
        acc_ref[0, :, hs[h]] = o
        c_ref[0, h] = jnp.broadcast_to(c, (t, LANES))
        m = jnp.max(c)
        cm = m if cm is None else jnp.maximum(cm, m)
    flag_ref[...] = jnp.full(flag_ref.shape, cm, F32)


def _sb_tail_kernel(q_ref, kc_ref, vc_ref, nu_ref, acc_in_ref, c_in_ref, acc_ref, acc_scr, c_scr):
    s = pl.program_id(2)

    @pl.when(s == 0)
    def _():
        for h in range(2):
            acc_scr[h] = acc_in_ref[0][:, h * SB_HEAD_DIM:(h + 1) * SB_HEAD_DIM]
            c_scr[h] = c_in_ref[0, h][:, 0:1]

    @pl.when(jnp.maximum(jnp.max(c_scr[0]), jnp.max(c_scr[1])) > SB_DEAD_LOG)
    def _():
        for h in range(2):
            qh = q_ref[0][:, h * SB_HEAD_DIM:(h + 1) * SB_HEAD_DIM]
            lb, ex, tot = _sb_scores(_dot(qh, kc_ref[0, h].astype(BF16)), nu_ref[...], None)
            c = c_scr[h]
            acc_scr[h] += _dot_nt(_sb_weights(lb, ex, c), vc_ref[0, h].astype(BF16))
            c_scr[h] = c + tot

    @pl.when(s == pl.num_programs(2) - 1)
    def _():
        acc_ref[0] = jnp.concatenate([acc_scr[0], acc_scr[1]], axis=1)


def _sb_sample(q, kn, vn, kct, vct):
    b, t, _ = q.shape
    past = kct.shape[3]
    tk = SB_BLOCK
    near = min(SB_SAMPLE_NEAR, past)
    hb = SB_SAMPLE_HEADS
    ng = SB_HEADS // hb
    new_spec = pl.BlockSpec((1, t, hb * SB_HEAD_DIM), lambda i, h: (i, 0, h))
    cache_spec = pl.BlockSpec((1, hb, SB_HEAD_DIM, near), lambda i, h: (i, h, 0, past // near - 1))
    nun, nup = _neg_upper(t), _neg_upper(tk)
    acc, c, flag = pl.pallas_call(
        functools.partial(_sb_sample_kernel, t=t, hb=hb, nblk=near // tk, tk=tk),
        grid=(b, ng),
        in_specs=[new_spec, new_spec, new_spec, cache_spec, cache_spec,
                  pl.BlockSpec((t, t), lambda i, h: (0, 0)),
                  pl.BlockSpec((tk, tk), lambda i, h: (0, 0))],
        out_specs=[new_spec,
                   pl.BlockSpec((1, hb, t, LANES), lambda i, h: (i, h, 0, 0)),
                   pl.BlockSpec((1, 1, SUBLANES, LANES), lambda i, h: (i, h, 0, 0))],
        out_shape=[jax.ShapeDtypeStruct((b, t, D_MODEL), F32),
                   jax.ShapeDtypeStruct((b, SB_HEADS, t, LANES), F32),
                   jax.ShapeDtypeStruct((b, ng, SUBLANES, LANES), F32)],
        compiler_params=_params(("arbitrary", "arbitrary")),
        name="sb_sample",
    )(q, kn, vn, kct, vct, nun, nup)
    nfar = (past - near) // tk
    if nfar == 0:
        return acc

    def tail():
        pair = pl.BlockSpec((1, t, 2 * SB_HEAD_DIM), lambda i, h, s: (i, 0, h))
        far = pl.BlockSpec((1, 2, SB_HEAD_DIM, tk), lambda i, h, s: (i, h, 0, nfar - 1 - s))
        return pl.pallas_call(
            _sb_tail_kernel,
            grid=(b, SB_HEADS // 2, nfar),
            in_specs=[pair, far, far, pl.BlockSpec((tk, tk), lambda i, h, s: (0, 0)), pair,
                      pl.BlockSpec((1, 2, t, LANES), lambda i, h, s: (i, h, 0, 0))],
            out_specs=pair,
            out_shape=jax.ShapeDtypeStruct((b, t, D_MODEL), F32),
            scratch_shapes=[pltpu.VMEM((2, t, SB_HEAD_DIM), F32), pltpu.VMEM((2, t, 1), F32)],
            compiler_params=_params(("arbitrary", "arbitrary", "arbitrary")),
            name="sb_tail",
        )(q, kct, vct, nup, acc, c)

    return lax.cond(jnp.max(flag) > SB_DEAD_LOG, tail, lambda: acc)


CONV_ROWS = 128
CONV_LANES = 128


def _mix_kernel(x_ref, osb_ref, left_ref, mk_ref, mv_ref, gpre_ref, wa_ref, wb_ref,
                wsb_ref, wcv_ref, wmo_ref, wout_ref, dww_ref, dwb_ref, lng_ref, lnb_ref, gpost_ref,
                y_ref, ulast_ref, hb_ref, uf_ref, cc_ref, qm_ref, om_ref, mg_ref, gb_ref, *, G, tt):
    s = pl.program_id(1)
    rows = G * tt
    H0 = CONV_HALO
    off = H0 - (CONV_WIDTH - 1)
    x = x_ref[...].reshape(rows, D_MODEL)
    hb_ref[...] = _rms(x, gpre_ref[...]).astype(BF16)

    def proj(w_ref, grp):
        return _dot(hb_ref[...], w_ref[:, grp * D_MODEL:(grp + 1) * D_MODEL])

    @pl.when(s == 0)
    def _():
        uf_ref[:, 0:H0, :] = left_ref[...]

    @pl.when(s > 0)
    def _():
        uf_ref[:, 0:H0, :] = uf_ref[:, tt:tt + H0, :]

    uf_ref[:, H0:H0 + tt, :] = (proj(wa_ref, 0) * _sigmoid(proj(wa_ref, 1))).reshape(G, tt, D_MODEL)
    ulast_ref[...] = uf_ref[:, tt:tt + H0, :]

    qm_ref[...] = (proj(wa_ref, 2) * (MEM_HEAD_DIM ** -0.5)).astype(BF16)
    for g in range(G):
        for h in range(MEM_HEADS):
            cs = slice(h * MEM_HEAD_DIM, (h + 1) * MEM_HEAD_DIM)
            sc = _dot_nt(qm_ref[g * tt:(g + 1) * tt, cs], mk_ref[g, h].astype(BF16))
            p = jnp.exp(sc - jnp.max(sc, axis=-1, keepdims=True))
            den = jnp.sum(p, axis=-1, keepdims=True)
            oh = _dot(p.astype(BF16), mv_ref[g, h].astype(BF16)) * (1.0 / den)
            om_ref[g * tt:(g + 1) * tt, cs] = oh.astype(BF16)
    y_sb = _dot(osb_ref[...].reshape(rows, D_MODEL).astype(BF16), wsb_ref[...])
    mg_ref[...] = _sigmoid(proj(wb_ref, 0)) * y_sb
    mg_ref[...] += _sigmoid(proj(wb_ref, 2)) * _dot(om_ref[...], wmo_ref[...])
    gb_ref[...] = _sigmoid(proj(wb_ref, 1))

    rc = min(CONV_ROWS, tt)
    win = rc + H0
    for g in range(G):
        for r0 in range(0, tt, rc):
            for c0 in range(0, D_MODEL, CONV_LANES):
                cs = slice(c0, c0 + CONV_LANES)
                window = uf_ref[g, r0:r0 + win, cs]
                acc = jnp.broadcast_to(dwb_ref[:, cs], (rc, CONV_LANES))
                for r in range(SUBLANES):
                    shifted = window if r == 0 else pltpu.roll(window, win - r, axis=0)
                    for m in range(H0 // SUBLANES + 1):
                        j = SUBLANES * m + r - off
                        if 0 <= j < CONV_WIDTH:
                            acc = acc + shifted[SUBLANES * m:SUBLANES * m + rc] * dww_ref[j:j + 1, cs]
                cc_ref[g * tt + r0:g * tt + r0 + rc, cs] = acc

    cc = cc_ref[...]
    mu = jnp.mean(cc, axis=-1, keepdims=True)
    d = cc - mu
    var = jnp.mean(d * d, axis=-1, keepdims=True)
    yn = d * lax.rsqrt(var + NORM_EPS) * lng_ref[...] + lnb_ref[...]
    y_conv = _dot((yn * _sigmoid(yn)).astype(BF16), wcv_ref[...])

    merged = mg_ref[...] + gb_ref[...] * y_conv
    out = _dot(merged.astype(BF16), wout_ref[...])
    y_ref[...] = (x + _rms(out, gpost_ref[...])).reshape(G, tt, D_MODEL)


def _mix(x, osb, left, mk, mv, gpre, w_in, wsb, wcv, wmo, wout, dww, dwb, lng, lnb, gpost, *, G, tt):
    b, t, _ = x.shape
    rows = G * tt
    tok = pl.BlockSpec((G, tt, D_MODEL), lambda i, s: (i, s, 0))
    halo = pl.BlockSpec((G, CONV_HALO, D_MODEL), lambda i, s: (i, 0, 0))
    mem = pl.BlockSpec((G, MEM_HEADS, MEM_TOKENS, MEM_HEAD_DIM), lambda i, s: (i, 0, 0, 0))
    wsq = _resident((D_MODEL, D_MODEL), lambda i, s: (0, 0))
    vec = pl.BlockSpec((1, D_MODEL), lambda i, s: (0, 0))
    return pl.pallas_call(
        functools.partial(_mix_kernel, G=G, tt=tt),
        grid=(b // G, t // tt),
        in_specs=[tok, tok, halo, mem, mem, vec,
                  _resident((D_MODEL, 3 * D_MODEL), lambda i, s: (0, 1)),
                  _resident((D_MODEL, 3 * D_MODEL), lambda i, s: (0, 2)),
                  wsq, wsq, wsq, wsq,
                  pl.BlockSpec((CONV_WIDTH, D_MODEL), lambda i, s: (0, 0)),
                  vec, vec, vec, vec],
        out_specs=[tok, halo],
        out_shape=[jax.ShapeDtypeStruct((b, t, D_MODEL), F32),
                   jax.ShapeDtypeStruct((b, CONV_HALO, D_MODEL), F32)],
        scratch_shapes=[pltpu.VMEM((rows, D_MODEL), BF16),
                        pltpu.VMEM((G, CONV_HALO + tt, D_MODEL), F32),
                        pltpu.VMEM((rows, D_MODEL), F32),
                        pltpu.VMEM((rows, D_MODEL), BF16),
                        pltpu.VMEM((rows, D_MODEL), BF16),
                        pltpu.VMEM((rows, D_MODEL), F32),
                        pltpu.VMEM((rows, D_MODEL), F32)],
        compiler_params=_params(("arbitrary", "arbitrary")),
        name="mix",
    )(x, osb, left, mk, mv, gpre, w_in, w_in, wsb, wcv, wmo, wout, dww, dwb, lng, lnb, gpost)


FFN_ROWS = 256
FFN_COLS = 512


def _gelu_tanh(x):
    return 0.5 * x * (1.0 + jnp.tanh(0.7978845608028654 * (x + 0.044715 * (x * x * x))))


def _ffn_kernel(x_ref, lg_ref, lv_ref, gpre_ref, gpost_ref, wg_ref, wv_ref, dwg_ref, dwv_ref, wd_ref,
                y_ref, st_ref, hb_ref, acc_ref, upf_ref, carry_ref, *, G, tt, nf):
    s = pl.program_id(1)
    f = pl.program_id(2)
    rows = G * tt
    S = SUBLANES
    lefts = (lg_ref, lv_ref)
    ws = (wg_ref, wv_ref)
    dws = (dwg_ref, dwv_ref)

    @pl.when(f == 0)
    def _():
        hb_ref[...] = _rms(x_ref[...].reshape(rows, D_MODEL), gpre_ref[...]).astype(BF16)
        acc_ref[...] = jnp.zeros_like(acc_ref)

    for half in range(2):
        @pl.when(s == 0)
        def _():
            upf_ref[half, :, 0:S, :] = lefts[half][...]

        @pl.when(s > 0)
        def _():
            upf_ref[half, :, 0:S, :] = carry_ref[f, half]

    rc = min(FFN_ROWS, rows)
    gpc = max(rc // tt, 1)
    tc = min(rc, tt)
    tf = wd_ref.shape[0]
    for r0 in range(0, rows, rc):
        g0, t0 = r0 // tt, r0 % tt
        hb = hb_ref[r0:r0 + rc, :]
        acts = []
        for c0 in range(0, tf, FFN_COLS):
            cs = slice(c0, min(c0 + FFN_COLS, tf))
            conv = []
            for half in range(2):
                up = _dot(hb, ws[half][:, cs]).reshape(gpc, tc, -1)
                upf_ref[half, g0:g0 + gpc, S + t0:S + t0 + tc, cs] = up
                dw = dws[half]
                c = (up * dw[2:3, cs]
                     + upf_ref[half, g0:g0 + gpc, S + t0 - 1:S + t0 - 1 + tc, cs] * dw[1:2, cs]
                     + upf_ref[half, g0:g0 + gpc, S + t0 - 2:S + t0 - 2 + tc, cs] * dw[0:1, cs])
                conv.append(c.reshape(rc, -1))
            acts.append((_gelu_tanh(conv[0]) * conv[1]).astype(BF16))
        acc_ref[r0:r0 + rc, :] += _dot(jnp.concatenate(acts, axis=1), wd_ref[...])

    for half in range(2):
        last = upf_ref[half, :, tt:tt + S, :]
        carry_ref[f, half] = last
        st_ref[half, f] = last

    @pl.when(f == nf - 1)
    def _():
        y = x_ref[...].reshape(rows, D_MODEL) + _rms(acc_ref[...], gpost_ref[...])
        y_ref[...] = y.reshape(G, tt, D_MODEL)


def _ffn(x, left, gpre, gpost, wup, dw, wdown, *, G, tt, nf):
    b, t, _ = x.shape
    rows = G * tt
    tf = FFN_DIM // nf
    S = SUBLANES
    tok = pl.BlockSpec((G, tt, D_MODEL), lambda i, s, f: (i, s, 0))
    vec = pl.BlockSpec((1, D_MODEL), lambda i, s, f: (0, 0))
    y, st = pl.pallas_call(
        functools.partial(_ffn_kernel, G=G, tt=tt, nf=nf),
        grid=(b // G, t // tt, nf),
        in_specs=[tok,
                  pl.BlockSpec((G, S, tf), lambda i, s, f: (i, 0, f)),
                  pl.BlockSpec((G, S, tf), lambda i, s, f: (i, 0, nf + f)),
                  vec, vec,
                  pl.BlockSpec((D_MODEL, tf), lambda i, s, f: (0, f)),
                  pl.BlockSpec((D_MODEL, tf), lambda i, s, f: (0, nf + f)),
                  pl.BlockSpec((FFN_CONV_WIDTH, tf), lambda i, s, f: (0, f)),
                  pl.BlockSpec((FFN_CONV_WIDTH, tf), lambda i, s, f: (0, nf + f)),
                  pl.BlockSpec((tf, D_MODEL), lambda i, s, f: (f, 0))],
        out_specs=[tok, pl.BlockSpec((2, nf, G, S, tf), lambda i, s, f: (0, 0, i, 0, 0))],
        out_shape=[jax.ShapeDtypeStruct((b, t, D_MODEL), F32),
                   jax.ShapeDtypeStruct((2, nf, b, S, tf), F32)],
        scratch_shapes=[pltpu.VMEM((rows, D_MODEL), BF16),
                        pltpu.VMEM((rows, D_MODEL), F32),
                        pltpu.VMEM((2, G, S + tt, tf), F32),
                        pltpu.VMEM((nf, 2, G, S, tf), F32)],
        compiler_params=_params(("arbitrary", "arbitrary", "arbitrary")),
        name="ffn",
    )(x, left, left, gpre, gpost, wup, wup, dw, dw, wdown)
    keep = FFN_CONV_WIDTH - 1
    state = st[:, :, :, S - keep:, :].transpose(2, 3, 0, 1, 4).reshape(b, keep, 2 * FFN_DIM)
    return y, state


def _tiles(b, t):
    if t >= 512:
        return {"qkv": (1, 512), "mix": (1, 256), "ffn": (1, 512)}
    return {"qkv": (min(b, 512 // t), t), "mix": (min(b, 256 // t), t), "ffn": (min(b, 512 // t), t)}


def _layer(x, mk, mv, sb_cache, conv_left, ffn_left, w):
    b, t, _ = x.shape
    tiles = _tiles(b, t)
    prompt = sb_cache is None
    G, tt = tiles["qkv"]
    q, k, v, kf, vf = _qkv(x, w["g_mix_pre"], w["w_in"], G=G, tt=tt, dmajor=prompt)
    if prompt:
        osb = _sb_prompt(q, k, v, tq=SB_BLOCK)
        kf = kf.reshape(b, SB_HEADS, SB_HEAD_DIM, t).swapaxes(2, 3)
        vf = vf.reshape(b, SB_HEADS, SB_HEAD_DIM, t).swapaxes(2, 3)
    else:
        osb = _sb_sample(q, k, v, sb_cache[0].swapaxes(2, 3), sb_cache[1].swapaxes(2, 3))

    keep = CONV_WIDTH - 1
    left = jnp.pad(conv_left, ((0, 0), (CONV_HALO - keep, 0), (0, 0)))
    G, tt = tiles["mix"]
    x1, ulast = _mix(x, osb, left, mk, mv, w["g_mix_pre"], w["w_in"], w["w_sb_o"], w["w_conv_o"], w["w_mem_o"],
                     w["w_out"], w["conv_dw_w"], w["conv_dw_b"], w["conv_ln_g"], w["conv_ln_b"], w["g_mix_post"],
                     G=G, tt=tt)
    conv_state = ulast[:, CONV_HALO - keep:]

    fkeep = FFN_CONV_WIDTH - 1
    fleft = jnp.pad(ffn_left, ((0, 0), (SUBLANES - fkeep, 0), (0, 0)))
    G, tt = tiles["ffn"]
    y, ffn_state = _ffn(x1, fleft, w["g_ffn_pre"], w["g_ffn_post"], w["w_ffn_up"], w["ffn_dw_w"],
                        w["w_ffn_down"], G=G, tt=tt, nf=2)
    return y, kf, vf, conv_state, ffn_state


def kernel(x_prompt, x_sample, mem_prompt, cache_sb_k, cache_sb_v, state_conv, state_ffn_conv, cache_mem_k, cache_mem_v, g_mem, w_mem_kv, g_mix_pre, g_mix_post, w_in, w_sb_o, conv_dw_w, conv_dw_b, conv_ln_g, conv_ln_b, w_conv_o, w_mem_o, w_out, g_ffn_pre, g_ffn_post, w_ffn_up, ffn_dw_w, w_ffn_down):
    depth = w_in.shape[0]
    bp = x_prompt.shape[0]
    yp, ys = x_prompt, x_sample
    outs = [[] for _ in range(10)]
    for l in range(depth):
        vec = lambda a: a[l].reshape(1, -1)
        w = {
            "g_mix_pre": vec(g_mix_pre), "g_mix_post": vec(g_mix_post),
            "w_in": w_in[l].astype(BF16), "w_sb_o": w_sb_o[l].astype(BF16),
            "conv_dw_w": conv_dw_w[l], "conv_dw_b": vec(conv_dw_b),
            "conv_ln_g": vec(conv_ln_g), "conv_ln_b": vec(conv_ln_b),
            "w_conv_o": w_conv_o[l].astype(BF16), "w_mem_o": w_mem_o[l].astype(BF16),
            "w_out": w_out[l].astype(BF16),
            "g_ffn_pre": vec(g_ffn_pre), "g_ffn_post": vec(g_ffn_post),
            "w_ffn_up": w_ffn_up[l].astype(BF16), "ffn_dw_w": ffn_dw_w[l],
            "w_ffn_down": w_ffn_down[l].astype(BF16),
        }
        mkv, mkv_bf = _memkv(mem_prompt, vec(g_mem), w_mem_kv[l].astype(BF16))
        yp, kp, vp, cp, fp = _layer(
            yp, mkv_bf[0], mkv_bf[1], None,
            jnp.zeros((bp, CONV_WIDTH - 1, D_MODEL), F32),
            jnp.zeros((bp, FFN_CONV_WIDTH - 1, 2 * FFN_DIM), F32), w)
        ys, ks, vs, cs, fs = _layer(
            ys, cache_mem_k[l], cache_mem_v[l], (cache_sb_k[l], cache_sb_v[l]),
            state_conv[l], state_ffn_conv[l], w)
        for lst, val in zip(outs, (kp, vp, ks, vs, cp, cs, fp, fs, mkv[0], mkv[1])):
            lst.append(val)
    return (yp, ys) + tuple(jnp.stack(o) for o in outs)
```

```python
import functools

import jax
import jax.numpy as jnp
from jax import lax
from jax.experimental import pallas as pl
from jax.experimental.pallas import tpu as pltpu

F32 = jnp.float32
BF16 = jnp.bfloat16

D_MODEL = 1024
SB_HEADS = 16
SB_HEAD_DIM = 64
MEM_HEADS = 4
MEM_HEAD_DIM = 256
MEM_TOKENS = 256
CONV_WIDTH = 31
FFN_DIM = 2816
FFN_CONV_WIDTH = 3
NORM_EPS = 1e-6

SUBLANES = 8
LANES = 128
CONV_HALO = 32
VMEM_LIMIT_BYTES = 56 * 1024 * 1024

SB_DEAD_LOG = -110.0
SB_MASKED_SCORE = -1e4
SB_BLOCK = 256
SB_PROMPT_HEADS = 16
SB_SAMPLE_HEADS = 8
SB_SAMPLE_NEAR = 512


def _rms(x, g):
    return x * lax.rsqrt(jnp.mean(x * x, axis=-1, keepdims=True) + NORM_EPS) * g


def _sigmoid(x):
    return 0.5 * jnp.tanh(0.5 * x) + 0.5


def _softplus(z):
    sign = jnp.uint32(0x80000000)
    neg_abs = lax.bitcast_convert_type(lax.bitcast_convert_type(z, jnp.uint32) | sign, F32)
    return jnp.maximum(z, 0.0) + jnp.log(1.0 + jnp.exp(neg_abs))


def _dot(a, b):
    return jnp.dot(a, b, preferred_element_type=F32)


def _dot_nt(a, b):
    return lax.dot_general(a, b, (((1,), (1,)), ((), ())), preferred_element_type=F32)


def _params(sem):
    return pltpu.CompilerParams(dimension_semantics=sem, vmem_limit_bytes=VMEM_LIMIT_BYTES)


def _resident(block, index_map):
    return pl.BlockSpec(block, index_map, pipeline_mode=pl.Buffered(1))


def _memkv_kernel(mem_ref, g_ref, w_ref, o_ref, ob_ref, hb_ref):
    @pl.when(pl.program_id(1) == 0)
    def _():
        hb_ref[...] = _rms(mem_ref[0], g_ref[...]).astype(BF16)

    res = _dot(hb_ref[...], w_ref[...])
    o_ref[0, 0, 0] = res
    ob_ref[0, 0, 0] = res.astype(BF16)


def _memkv(mem, g, w_bf):
    b = mem.shape[0]
    nj = 2 * MEM_HEADS
    out_map = lambda i, j: (j // MEM_HEADS, i, j % MEM_HEADS, 0, 0)
    shape = (2, b, MEM_HEADS, MEM_TOKENS, MEM_HEAD_DIM)
    return pl.pallas_call(
        _memkv_kernel,
        grid=(b, nj),
        in_specs=[
            pl.BlockSpec((1, MEM_TOKENS, D_MODEL), lambda i, j: (i, 0, 0)),
            pl.BlockSpec((1, D_MODEL), lambda i, j: (0, 0)),
            pl.BlockSpec((D_MODEL, MEM_HEAD_DIM), lambda i, j: (0, j)),
        ],
        out_specs=[
            pl.BlockSpec((1, 1, 1, MEM_TOKENS, MEM_HEAD_DIM), out_map),
            pl.BlockSpec((1, 1, 1, MEM_TOKENS, MEM_HEAD_DIM), out_map),
        ],
        out_shape=[jax.ShapeDtypeStruct(shape, F32), jax.ShapeDtypeStruct(shape, BF16)],
        scratch_shapes=[pltpu.VMEM((MEM_TOKENS, D_MODEL), BF16)],
        compiler_params=_params(("arbitrary", "arbitrary")),
        name="memkv",
    )(mem, g, w_bf)


QKV_COLS = 512


def _qkv_kernel(x_ref, g_ref, w_ref, q_ref, k_ref, v_ref, kf_ref, vf_ref, hb_ref, *, G, tt, dmajor):
    rows = G * tt
    cc = QKV_COLS
    hb_ref[...] = _rms(x_ref[...].reshape(rows, D_MODEL), g_ref[...]).astype(BF16)

    def tok(ref, n0, val):
        ref[:, :, n0:n0 + cc] = val.reshape(G, tt, cc).astype(ref.dtype)

    def heads(ref, n0, res):
        if dmajor:
            ref[0, n0:n0 + cc, :] = res.T
        else:
            for g in range(G):
                for hh in range(cc // SB_HEAD_DIM):
                    ref[g, n0 // SB_HEAD_DIM + hh] = res[g * tt:(g + 1) * tt, hh * SB_HEAD_DIM:(hh + 1) * SB_HEAD_DIM]

    for grp, (tref, href) in enumerate(((q_ref, None), (k_ref, kf_ref), (v_ref, vf_ref))):
        for n0 in range(0, D_MODEL, cc):
            res = _dot(hb_ref[...], w_ref[:, grp * D_MODEL + n0:grp * D_MODEL + n0 + cc])
            tok(tref, n0, res * (SB_HEAD_DIM ** -0.5) if href is None else res)
            if href is not None:
                heads(href, n0, res)


def _qkv(x, g, w_bf, *, G, tt, dmajor):
    b, t, _ = x.shape
    rows = G * tt
    tok_spec = pl.BlockSpec((G, tt, D_MODEL), lambda i, s: (i, s, 0))
    if dmajor:
        head_spec = pl.BlockSpec((G, D_MODEL, tt), lambda i, s: (i, 0, s))
        head_shape = jax.ShapeDtypeStruct((b, D_MODEL, t), F32)
    else:
        head_spec = pl.BlockSpec((G, SB_HEADS, tt, SB_HEAD_DIM), lambda i, s: (i, 0, s, 0))
        head_shape = jax.ShapeDtypeStruct((b, SB_HEADS, t, SB_HEAD_DIM), F32)
    tok_bf = jax.ShapeDtypeStruct((b, t, D_MODEL), BF16)
    return pl.pallas_call(
        functools.partial(_qkv_kernel, G=G, tt=tt, dmajor=dmajor),
        grid=(b // G, t // tt),
        in_specs=[tok_spec,
                  pl.BlockSpec((1, D_MODEL), lambda i, s: (0, 0)),
                  _resident((D_MODEL, 3 * D_MODEL), lambda i, s: (0, 0))],
        out_specs=[tok_spec, tok_spec, tok_spec, head_spec, head_spec],
        out_shape=[tok_bf, tok_bf, tok_bf, head_shape, head_shape],
        scratch_shapes=[pltpu.VMEM((rows, D_MODEL), BF16)],
        compiler_params=_params(("arbitrary", "arbitrary")),
        name="qkv",
    )(x, g, w_bf)


def _neg_upper(n):
    j = lax.broadcasted_iota(jnp.int32, (n, n), 0)
    s = lax.broadcasted_iota(jnp.int32, (n, n), 1)
    return jnp.where(j > s, -1.0, 0.0).astype(BF16)


def _strict_lower(n):
    r = lax.broadcasted_iota(jnp.int32, (n, n), 0)
    s = lax.broadcasted_iota(jnp.int32, (n, n), 1)
    return s < r


def _sb_mask(z, mask):
    return z if mask is None else jnp.where(mask, z, SB_MASKED_SCORE)


def _sb_scores(z, nu, mask):
    z = _sb_mask(z, mask)
    nl = _softplus(z)
    log_beta = z - nl
    excl = _dot(nl.astype(BF16), nu)
    tot = excl[:, 0:1] - nl[:, 0:1]
    return log_beta, excl, tot


def _sb_weights(log_beta, excl, c):
    return jnp.exp(log_beta + excl + c).astype(BF16)


def _sb_prompt_kernel(q_ref, k_ref, v_ref, nu_ref, o_ref, acc_ref, c_ref, *, tq):
    qi = pl.program_id(2)
    nu = nu_ref[...]
    mask = _strict_lower(tq)
    q = q_ref[0]
    hsl = [slice(h * SB_HEAD_DIM, (h + 1) * SB_HEAD_DIM) for h in range(2)]

    def kv(kb):
        start = pl.multiple_of(kb * tq, tq)
        return k_ref[0, pl.ds(start, tq), :], v_ref[0, pl.ds(start, tq), :]

    def cmax():
        return jnp.maximum(jnp.max(c_ref[0]), jnp.max(c_ref[1]))

    @pl.when(qi == 0)
    def _():
        kd, vd = kv(qi)
        for h in range(2):
            lb, ex, tot = _sb_scores(_dot_nt(q[:, hsl[h]], kd[:, hsl[h]]), nu, mask)
            acc_ref[h] = _dot(_sb_weights(lb, ex, 0.0), vd[:, hsl[h]])
            c_ref[h] = tot

    @pl.when(qi > 0)
    def _():
        kd, vd = kv(qi)
        kp, vp = kv(qi - 1)
        for h in range(2):
            lb, ex, tot = _sb_scores(_dot_nt(q[:, hsl[h]], kd[:, hsl[h]]), nu, mask)
            lbp, exp_, totp = _sb_scores(_dot_nt(q[:, hsl[h]], kp[:, hsl[h]]), nu, None)
            o = _dot(_sb_weights(lb, ex, 0.0), vd[:, hsl[h]])
            o = o + _dot(_sb_weights(lbp, exp_, tot), vp[:, hsl[h]])
            acc_ref[h] = o
            c_ref[h] = tot + totp

    def cond(carry):
        kb, cm = carry
        return jnp.logical_and(kb >= 0, cm > SB_DEAD_LOG)

    def body(carry):
        kb, _ = carry
        kk, vv = kv(kb)
        for h in range(2):
            lb, ex, tot = _sb_scores(_dot_nt(q[:, hsl[h]], kk[:, hsl[h]]), nu, None)
            c = c_ref[h]
            acc_ref[h] += _dot(_sb_weights(lb, ex, c), vv[:, hsl[h]])
            c_ref[h] = c + tot
        return kb - 1, cmax()

    lax.while_loop(cond, body, (qi - 2, cmax()))
    o_ref[0] = jnp.concatenate([acc_ref[0], acc_ref[1]], axis=1).astype(BF16)


def _sb_near_kernel(q_ref, kd_ref, kp_ref, vd_ref, vp_ref, nu_ref, o_ref, flag_ref, *, tq, hb):
    qi = pl.program_id(2)
    nu = nu_ref[...]
    mask = _strict_lower(tq)
    q, kd, vd = q_ref[0], kd_ref[0], vd_ref[0]
    hs = [slice(h * SB_HEAD_DIM, (h + 1) * SB_HEAD_DIM) for h in range(hb)]

    @pl.when(qi == 0)
    def _():
        outs = []
        for h in range(hb):
            lb, ex, _ = _sb_scores(_dot_nt(q[:, hs[h]], kd[:, hs[h]]), nu, mask)
            outs.append(_dot(_sb_weights(lb, ex, 0.0), vd[:, hs[h]]))
        o_ref[0] = jnp.concatenate(outs, axis=1).astype(BF16)
        flag_ref[...] = jnp.full(flag_ref.shape, 2.0 * SB_DEAD_LOG, F32)

    @pl.when(qi > 0)
    def _():
        kp, vp = kp_ref[0], vp_ref[0]
        outs = []
        cm = None
        for h in range(hb):
            lb, ex, tot = _sb_scores(_dot_nt(q[:, hs[h]], kd[:, hs[h]]), nu, mask)
            lbp, exp_, totp = _sb_scores(_dot_nt(q[:, hs[h]], kp[:, hs[h]]), nu, None)
            o = _dot(_sb_weights(lb, ex, 0.0), vd[:, hs[h]])
            outs.append(o + _dot(_sb_weights(lbp, exp_, tot), vp[:, hs[h]]))
            m = jnp.max(tot + totp)
            cm = m if cm is None else jnp.maximum(cm, m)
        o_ref[0] = jnp.concatenate(outs, axis=1).astype(BF16)
        flag_ref[...] = jnp.full(flag_ref.shape, jnp.where(qi > 1, cm, 2.0 * SB_DEAD_LOG), F32)


def _sb_prompt(q, k, v, *, tq):
    b, t, _ = q.shape
    hb = SB_PROMPT_HEADS
    ng = SB_HEADS // hb
    nq = t // tq
    lanes = hb * SB_HEAD_DIM
    cur = pl.BlockSpec((1, tq, lanes), lambda i, h, s: (i, s, h))
    prev = pl.BlockSpec((1, tq, lanes), lambda i, h, s: (i, jnp.maximum(s - 1, 0), h))
    o, flag = pl.pallas_call(
        functools.partial(_sb_near_kernel, tq=tq, hb=hb),
        grid=(b, ng, nq),
        in_specs=[cur, cur, prev, cur, prev, pl.BlockSpec((tq, tq), lambda i, h, s: (0, 0))],
        out_specs=[cur, pl.BlockSpec((1, 1, 1, SUBLANES, LANES), lambda i, h, s: (i, h, s, 0, 0))],
        out_shape=[jax.ShapeDtypeStruct((b, t, D_MODEL), BF16),
                   jax.ShapeDtypeStruct((b, ng, nq, SUBLANES, LANES), F32)],
        compiler_params=_params(("arbitrary", "arbitrary", "arbitrary")),
        name="sb_near",
    )(q, k, k, v, v, _neg_upper(tq))
    return lax.cond(jnp.max(flag) > SB_DEAD_LOG, lambda: _sb_prompt_full(q, k, v, tq=tq), lambda: o)


def _sb_prompt_full(q, k, v, *, tq):
    b, t, _ = q.shape
    hp = SB_HEADS // 2
    lanes = 2 * SB_HEAD_DIM
    nu = _neg_upper(tq)
    return pl.pallas_call(
        functools.partial(_sb_prompt_kernel, tq=tq),
        grid=(b, hp, t // tq),
        in_specs=[
            pl.BlockSpec((1, tq, lanes), lambda i, h, s: (i, s, h)),
            pl.BlockSpec((1, t, lanes), lambda i, h, s: (i, 0, h)),
            pl.BlockSpec((1, t, lanes), lambda i, h, s: (i, 0, h)),
            pl.BlockSpec((tq, tq), lambda i, h, s: (0, 0)),
        ],
        out_specs=pl.BlockSpec((1, tq, lanes), lambda i, h, s: (i, s, h)),
        out_shape=jax.ShapeDtypeStruct((b, t, D_MODEL), BF16),
        scratch_shapes=[pltpu.VMEM((2, tq, SB_HEAD_DIM), F32), pltpu.VMEM((2, tq, 1), F32)],
        compiler_params=_params(("arbitrary", "arbitrary", "arbitrary")),
        name="sb_prompt",
    )(q, k, v, nu)


def _sb_sample_kernel(q_ref, kn_ref, vn_ref, kc_ref, vc_ref, nun_ref, nup_ref, acc_ref, c_ref, flag_ref,
                      *, t, hb, nblk, tk):
    mask = _strict_lower(t)
    q, kn, vn = q_ref[0], kn_ref[0], vn_ref[0]
    hs = [slice(h * SB_HEAD_DIM, (h + 1) * SB_HEAD_DIM) for h in range(hb)]
    ks = [slice(kb * tk, (kb + 1) * tk) for kb in range(nblk - 1, -1, -1)]

    lbs, nls = [], []
    for h in range(hb):
        qh = q[:, hs[h]]
        zs = [_sb_mask(_dot_nt(qh, kn[:, hs[h]]), mask)]
        zs += [_dot(qh, kc_ref[0, h, :, s].astype(BF16)) for s in ks]
        nl = [_softplus(z) for z in zs]
        lbs.append([z - n for z, n in zip(zs, nl)])
        nls.append(nl)

    excl = []
    for sg in range(nblk + 1):
        stack = jnp.concatenate([nls[h][sg] for h in range(hb)], axis=0).astype(BF16)
        excl.append(_dot(stack, nun_ref[...] if sg == 0 else nup_ref[...]))

    cm = None
    for h in range(hb):
        rs = slice(h * t, (h + 1) * t)
        ex = excl[0][rs]
        o = _dot(_sb_weights(lbs[h][0], ex, 0.0), vn[:, hs[h]])
        c = ex[:, 0:1] - nls[h][0][:, 0:1]
        for i, s in enumerate(ks):
            ex = excl[i + 1][rs]
            o = o + _dot_nt(_sb_weights(lbs[h][i + 1], ex, c), vc_ref[0, h, :, s].astype(BF16))
            c = c + (ex[:, 0:1] - nls[h][i + 1][:, 0:1])
        acc_ref[0, :, hs[h]] = o
        c_ref[0, h] = jnp.broadcast_to(c, (t, LANES))
        m = jnp.max(c)
        cm = m if cm is None else jnp.maximum(cm, m)
    flag_ref[...] = jnp.full(flag_ref.shape, cm, F32)


def _sb_tail_kernel(q_ref, kc_ref, vc_ref, nu_ref, acc_in_ref, c_in_ref, acc_ref, acc_scr, c_scr):
    s = pl.program_id(2)

    @pl.when(s == 0)
    def _():
        for h in range(2):
            acc_scr[h] = acc_in_ref[0][:, h * SB_HEAD_DIM:(h + 1) * SB_HEAD_DIM]
            c_scr[h] = c_in_ref[0, h][:, 0:1]

    @pl.when(jnp.maximum(jnp.max(c_scr[0]), jnp.max(c_scr[1])) > SB_DEAD_LOG)
    def _():
        for h in range(2):
            qh = q_ref[0][:, h * SB_HEAD_DIM:(h + 1) * SB_HEAD_DIM]
            lb, ex, tot = _sb_scores(_dot(qh, kc_ref[0, h].astype(BF16)), nu_ref[...], None)
            c = c_scr[h]
            acc_scr[h] += _dot_nt(_sb_weights(lb, ex, c), vc_ref[0, h].astype(BF16))
            c_scr[h] = c + tot

    @pl.when(s == pl.num_programs(2) - 1)
    def _():
        acc_ref[0] = jnp.concatenate([acc_scr[0], acc_scr[1]], axis=1)


def _sb_sample(q, kn, vn, kct, vct):
    b, t, _ = q.shape
    past = kct.shape[3]
    tk = SB_BLOCK
    near = min(SB_SAMPLE_NEAR, past)
    hb = SB_SAMPLE_HEADS
    ng = SB_HEADS // hb
    new_spec = pl.BlockSpec((1, t, hb * SB_HEAD_DIM), lambda i, h: (i, 0, h))
    cache_spec = pl.BlockSpec((1, hb, SB_HEAD_DIM, near), lambda i, h: (i, h, 0, past // near - 1))
    nun, nup = _neg_upper(t), _neg_upper(tk)
    acc, c, flag = pl.pallas_call(
        functools.partial(_sb_sample_kernel, t=t, hb=hb, nblk=near // tk, tk=tk),
        grid=(b, ng),
        in_specs=[new_spec, new_spec, new_spec, cache_spec, cache_spec,
                  pl.BlockSpec((t, t), lambda i, h: (0, 0)),
                  pl.BlockSpec((tk, tk), lambda i, h: (0, 0))],
        out_specs=[new_spec,
                   pl.BlockSpec((1, hb, t, LANES), lambda i, h: (i, h, 0, 0)),
                   pl.BlockSpec((1, 1, SUBLANES, LANES), lambda i, h: (i, h, 0, 0))],
        out_shape=[jax.ShapeDtypeStruct((b, t, D_MODEL), F32),
                   jax.ShapeDtypeStruct((b, SB_HEADS, t, LANES), F32),
                   jax.ShapeDtypeStruct((b, ng, SUBLANES, LANES), F32)],
        compiler_params=_params(("arbitrary", "arbitrary")),
        name="sb_sample",
    )(q, kn, vn, kct, vct, nun, nup)
    nfar = (past - near) // tk
    if nfar == 0:
        return acc

    def tail():
        pair = pl.BlockSpec((1, t, 2 * SB_HEAD_DIM), lambda i, h, s: (i, 0, h))
        far = pl.BlockSpec((1, 2, SB_HEAD_DIM, tk), lambda i, h, s: (i, h, 0, nfar - 1 - s))
        return pl.pallas_call(
            _sb_tail_kernel,
            grid=(b, SB_HEADS // 2, nfar),
            in_specs=[pair, far, far, pl.BlockSpec((tk, tk), lambda i, h, s: (0, 0)), pair,
                      pl.BlockSpec((1, 2, t, LANES), lambda i, h, s: (i, h, 0, 0))],
            out_specs=pair,
            out_shape=jax.ShapeDtypeStruct((b, t, D_MODEL), F32),
            scratch_shapes=[pltpu.VMEM((2, t, SB_HEAD_DIM), F32), pltpu.VMEM((2, t, 1), F32)],
            compiler_params=_params(("arbitrary", "arbitrary", "arbitrary")),
            name="sb_tail",
        )(q, kct, vct, nup, acc, c)

    return lax.cond(jnp.max(flag) > SB_DEAD_LOG, tail, lambda: acc)


CONV_ROWS = 128
CONV_LANES = 128


def _mix_kernel(x_ref, osb_ref, left_ref, mk_ref, mv_ref, gpre_ref, wa_ref, wb_ref,
                wsb_ref, wcv_ref, wmo_ref, wout_ref, dww_ref, dwb_ref, lng_ref, lnb_ref, gpost_ref,
                y_ref, ulast_ref, hb_ref, uf_ref, cc_ref, qm_ref, om_ref, mg_ref, gb_ref, *, G, tt):
    s = pl.program_id(1)
    rows = G * tt
    H0 = CONV_HALO
    off = H0 - (CONV_WIDTH - 1)
    x = x_ref[...].reshape(rows, D_MODEL)
    hb_ref[...] = _rms(x, gpre_ref[...]).astype(BF16)

    def proj(w_ref, grp):
        return _dot(hb_ref[...], w_ref[:, grp * D_MODEL:(grp + 1) * D_MODEL])

    @pl.when(s == 0)
    def _():
        uf_ref[:, 0:H0, :] = left_ref[...]

    @pl.when(s > 0)
    def _():
        uf_ref[:, 0:H0, :] = uf_ref[:, tt:tt + H0, :]

    uf_ref[:, H0:H0 + tt, :] = (proj(wa_ref, 0) * _sigmoid(proj(wa_ref, 1))).reshape(G, tt, D_MODEL)
    ulast_ref[...] = uf_ref[:, tt:tt + H0, :]

    qm_ref[...] = (proj(wa_ref, 2) * (MEM_HEAD_DIM ** -0.5)).astype(BF16)
    for g in range(G):
        for h in range(MEM_HEADS):
            cs = slice(h * MEM_HEAD_DIM, (h + 1) * MEM_HEAD_DIM)
            sc = _dot_nt(qm_ref[g * tt:(g + 1) * tt, cs], mk_ref[g, h].astype(BF16))
            p = jnp.exp(sc - jnp.max(sc, axis=-1, keepdims=True))
            den = jnp.sum(p, axis=-1, keepdims=True)
            oh = _dot(p.astype(BF16), mv_ref[g, h].astype(BF16)) * (1.0 / den)
            om_ref[g * tt:(g + 1) * tt, cs] = oh.astype(BF16)
    y_sb = _dot(osb_ref[...].reshape(rows, D_MODEL).astype(BF16), wsb_ref[...])
    mg_ref[...] = _sigmoid(proj(wb_ref, 0)) * y_sb
    mg_ref[...] += _sigmoid(proj(wb_ref, 2)) * _dot(om_ref[...], wmo_ref[...])
    gb_ref[...] = _sigmoid(proj(wb_ref, 1))

    rc = min(CONV_ROWS, tt)
    win = rc + H0
    for g in range(G):
        for r0 in range(0, tt, rc):
            for c0 in range(0, D_MODEL, CONV_LANES):
                cs = slice(c0, c0 + CONV_LANES)
                window = uf_ref[g, r0:r0 + win, cs]
                acc = jnp.broadcast_to(dwb_ref[:, cs], (rc, CONV_LANES))
                for r in range(SUBLANES):
                    shifted = window if r == 0 else pltpu.roll(window, win - r, axis=0)
                    for m in range(H0 // SUBLANES + 1):
                        j = SUBLANES * m + r - off
                        if 0 <= j < CONV_WIDTH:
                            acc = acc + shifted[SUBLANES * m:SUBLANES * m + rc] * dww_ref[j:j + 1, cs]
                cc_ref[g * tt + r0:g * tt + r0 + rc, cs] = acc

    cc = cc_ref[...]
    mu = jnp.mean(cc, axis=-1, keepdims=True)
    d = cc - mu
    var = jnp.mean(d * d, axis=-1, keepdims=True)
    yn = d * lax.rsqrt(var + NORM_EPS) * lng_ref[...] + lnb_ref[...]
    y_conv = _dot((yn * _sigmoid(yn)).astype(BF16), wcv_ref[...])

    merged = mg_ref[...] + gb_ref[...] * y_conv
    out = _dot(merged.astype(BF16), wout_ref[...])
    y_ref[...] = (x + _rms(out, gpost_ref[...])).reshape(G, tt, D_MODEL)


def _mix(x, osb, left, mk, mv, gpre, w_in, wsb, wcv, wmo, wout, dww, dwb, lng, lnb, gpost, *, G, tt):
    b, t, _ = x.shape
    rows = G * tt
    tok = pl.BlockSpec((G, tt, D_MODEL), lambda i, s: (i, s, 0))
    halo = pl.BlockSpec((G, CONV_HALO, D_MODEL), lambda i, s: (i, 0, 0))
    mem = pl.BlockSpec((G, MEM_HEADS, MEM_TOKENS, MEM_HEAD_DIM), lambda i, s: (i, 0, 0, 0))
    wsq = _resident((D_MODEL, D_MODEL), lambda i, s: (0, 0))
    vec = pl.BlockSpec((1, D_MODEL), lambda i, s: (0, 0))
    return pl.pallas_call(
        functools.partial(_mix_kernel, G=G, tt=tt),
        grid=(b // G, t // tt),
        in_specs=[tok, tok, halo, mem, mem, vec,
                  _resident((D_MODEL, 3 * D_MODEL), lambda i, s: (0, 1)),
                  _resident((D_MODEL, 3 * D_MODEL), lambda i, s: (0, 2)),
                  wsq, wsq, wsq, wsq,
                  pl.BlockSpec((CONV_WIDTH, D_MODEL), lambda i, s: (0, 0)),
                  vec, vec, vec, vec],
        out_specs=[tok, halo],
        out_shape=[jax.ShapeDtypeStruct((b, t, D_MODEL), F32),
                   jax.ShapeDtypeStruct((b, CONV_HALO, D_MODEL), F32)],
        scratch_shapes=[pltpu.VMEM((rows, D_MODEL), BF16),
                        pltpu.VMEM((G, CONV_HALO + tt, D_MODEL), F32),
                        pltpu.VMEM((rows, D_MODEL), F32),
                        pltpu.VMEM((rows, D_MODEL), BF16),
                        pltpu.VMEM((rows, D_MODEL), BF16),
                        pltpu.VMEM((rows, D_MODEL), F32),
                        pltpu.VMEM((rows, D_MODEL), F32)],
        compiler_params=_params(("arbitrary", "arbitrary")),
        name="mix",
    )(x, osb, left, mk, mv, gpre, w_in, w_in, wsb, wcv, wmo, wout, dww, dwb, lng, lnb, gpost)


FFN_ROWS = 256
FFN_COLS = 512


def _gelu_tanh(x):
    return 0.5 * x * (1.0 + jnp.tanh(0.7978845608028654 * (x + 0.044715 * (x * x * x))))


def _ffn_kernel(x_ref, lg_ref, lv_ref, gpre_ref, gpost_ref, wg_ref, wv_ref, dwg_ref, dwv_ref, wd_ref,
                y_ref, st_ref, hb_ref, acc_ref, upf_ref, carry_ref, *, G, tt, nf):
    s = pl.program_id(1)
    f = pl.program_id(2)
    rows = G * tt
    S = SUBLANES
    lefts = (lg_ref, lv_ref)
    ws = (wg_ref, wv_ref)
    dws = (dwg_ref, dwv_ref)

    @pl.when(f == 0)
    def _():
        hb_ref[...] = _rms(x_ref[...].reshape(rows, D_MODEL), gpre_ref[...]).astype(BF16)
        acc_ref[...] = jnp.zeros_like(acc_ref)

    for half in range(2):
        @pl.when(s == 0)
        def _():
            upf_ref[half, :, 0:S, :] = lefts[half][...]

        @pl.when(s > 0)
        def _():
            upf_ref[half, :, 0:S, :] = carry_ref[f, half]

    rc = min(FFN_ROWS, rows)
    gpc = max(rc // tt, 1)
    tc = min(rc, tt)
    tf = wd_ref.shape[0]
    for r0 in range(0, rows, rc):
        g0, t0 = r0 // tt, r0 % tt
        hb = hb_ref[r0:r0 + rc, :]
        acts = []
        for c0 in range(0, tf, FFN_COLS):
            cs = slice(c0, min(c0 + FFN_COLS, tf))
            conv = []
            for half in range(2):
                up = _dot(hb, ws[half][:, cs]).reshape(gpc, tc, -1)
                upf_ref[half, g0:g0 + gpc, S + t0:S + t0 + tc, cs] = up
                dw = dws[half]
                c = (up * dw[2:3, cs]
                     + upf_ref[half, g0:g0 + gpc, S + t0 - 1:S + t0 - 1 + tc, cs] * dw[1:2, cs]
                     + upf_ref[half, g0:g0 + gpc, S + t0 - 2:S + t0 - 2 + tc, cs] * dw[0:1, cs])
                conv.append(c.reshape(rc, -1))
            acts.append((_gelu_tanh(conv[0]) * conv[1]).astype(BF16))
        acc_ref[r0:r0 + rc, :] += _dot(jnp.concatenate(acts, axis=1), wd_ref[...])

    for half in range(2):
        last = upf_ref[half, :, tt:tt + S, :]
        carry_ref[f, half] = last
        st_ref[half, f] = last

    @pl.when(f == nf - 1)
    def _():
        y = x_ref[...].reshape(rows, D_MODEL) + _rms(acc_ref[...], gpost_ref[...])
        y_ref[...] = y.reshape(G, tt, D_MODEL)


def _ffn(x, left, gpre, gpost, wup, dw, wdown, *, G, tt, nf):
    b, t, _ = x.shape
    rows = G * tt
    tf = FFN_DIM // nf
    S = SUBLANES
    tok = pl.BlockSpec((G, tt, D_MODEL), lambda i, s, f: (i, s, 0))
    vec = pl.BlockSpec((1, D_MODEL), lambda i, s, f: (0, 0))
    wspec = _resident if nf == 1 else pl.BlockSpec
    y, st = pl.pallas_call(
        functools.partial(_ffn_kernel, G=G, tt=tt, nf=nf),
        grid=(b // G, t // tt, nf),
        in_specs=[tok,
                  pl.BlockSpec((G, S, tf), lambda i, s, f: (i, 0, f)),
                  pl.BlockSpec((G, S, tf), lambda i, s, f: (i, 0, nf + f)),
                  vec, vec,
                  wspec((D_MODEL, tf), lambda i, s, f: (0, f)),
                  wspec((D_MODEL, tf), lambda i, s, f: (0, nf + f)),
                  pl.BlockSpec((FFN_CONV_WIDTH, tf), lambda i, s, f: (0, f)),
                  pl.BlockSpec((FFN_CONV_WIDTH, tf), lambda i, s, f: (0, nf + f)),
                  wspec((tf, D_MODEL), lambda i, s, f: (f, 0))],
        out_specs=[tok, pl.BlockSpec((2, nf, G, S, tf), lambda i, s, f: (0, 0, i, 0, 0))],
        out_shape=[jax.ShapeDtypeStruct((b, t, D_MODEL), F32),
                   jax.ShapeDtypeStruct((2, nf, b, S, tf), F32)],
        scratch_shapes=[pltpu.VMEM((rows, D_MODEL), BF16),
                        pltpu.VMEM((rows, D_MODEL), F32),
                        pltpu.VMEM((2, G, S + tt, tf), F32),
                        pltpu.VMEM((nf, 2, G, S, tf), F32)],
        compiler_params=_params(("arbitrary", "arbitrary", "arbitrary")),
        name="ffn",
    )(x, left, left, gpre, gpost, wup, wup, dw, dw, wdown)
    keep = FFN_CONV_WIDTH - 1
    state = st[:, :, :, S - keep:, :].transpose(2, 3, 0, 1, 4).reshape(b, keep, 2 * FFN_DIM)
    return y, state


def _tiles(b, t):
    if t >= 512:
        return {"qkv": (1, 512), "mix": (1, 512), "ffn": (1, 512)}
    return {"qkv": (min(b, 512 // t), t), "mix": (min(b, 256 // t), t), "ffn": (min(b, 512 // t), t)}


def _layer(x, mk, mv, sb_cache, conv_left, ffn_left, w):
    b, t, _ = x.shape
    tiles = _tiles(b, t)
    prompt = sb_cache is None
    G, tt = tiles["qkv"]
    q, k, v, kf, vf = _qkv(x, w["g_mix_pre"], w["w_in"], G=G, tt=tt, dmajor=prompt)
    if prompt:
        osb = _sb_prompt(q, k, v, tq=SB_BLOCK)
        kf = kf.reshape(b, SB_HEADS, SB_HEAD_DIM, t).swapaxes(2, 3)
        vf = vf.reshape(b, SB_HEADS, SB_HEAD_DIM, t).swapaxes(2, 3)
    else:
        osb = _sb_sample(q, k, v, sb_cache[0].swapaxes(2, 3), sb_cache[1].swapaxes(2, 3))

    keep = CONV_WIDTH - 1
    left = jnp.pad(conv_left, ((0, 0), (CONV_HALO - keep, 0), (0, 0)))
    G, tt = tiles["mix"]
    x1, ulast = _mix(x, osb, left, mk, mv, w["g_mix_pre"], w["w_in"], w["w_sb_o"], w["w_conv_o"], w["w_mem_o"],
                     w["w_out"], w["conv_dw_w"], w["conv_dw_b"], w["conv_ln_g"], w["conv_ln_b"], w["g_mix_post"],
                     G=G, tt=tt)
    conv_state = ulast[:, CONV_HALO - keep:]

    fkeep = FFN_CONV_WIDTH - 1
    fleft = jnp.pad(ffn_left, ((0, 0), (SUBLANES - fkeep, 0), (0, 0)))
    G, tt = tiles["ffn"]
    y, ffn_state = _ffn(x1, fleft, w["g_ffn_pre"], w["g_ffn_post"], w["w_ffn_up"], w["ffn_dw_w"],
                        w["w_ffn_down"], G=G, tt=tt, nf=1)
    return y, kf, vf, conv_state, ffn_state


def kernel(x_prompt, x_sample, mem_prompt, cache_sb_k, cache_sb_v, state_conv, state_ffn_conv, cache_mem_k, cache_mem_v, g_mem, w_mem_kv, g_mix_pre, g_mix_post, w_in, w_sb_o, conv_dw_w, conv_dw_b, conv_ln_g, conv_ln_b, w_conv_o, w_mem_o, w_out, g_ffn_pre, g_ffn_post, w_ffn_up, ffn_dw_w, w_ffn_down):
    depth = w_in.shape[0]
    bp = x_prompt.shape[0]
    yp, ys = x_prompt, x_sample
    outs = [[] for _ in range(10)]
    for l in range(depth):
        vec = lambda a: a[l].reshape(1, -1)
        w = {
            "g_mix_pre": vec(g_mix_pre), "g_mix_post": vec(g_mix_post),
            "w_in": w_in[l].astype(BF16), "w_sb_o": w_sb_o[l].astype(BF16),
            "conv_dw_w": conv_dw_w[l], "conv_dw_b": vec(conv_dw_b),
            "conv_ln_g": vec(conv_ln_g), "conv_ln_b": vec(conv_ln_b),
            "w_conv_o": w_conv_o[l].astype(BF16), "w_mem_o": w_mem_o[l].astype(BF16),
            "w_out": w_out[l].astype(BF16),
            "g_ffn_pre": vec(g_ffn_pre), "g_ffn_post": vec(g_ffn_post),
            "w_ffn_up": w_ffn_up[l].astype(BF16), "ffn_dw_w": ffn_dw_w[l],
            "w_ffn_down": w_ffn_down[l].astype(BF16),
        }
        mkv, mkv_bf = _memkv(mem_prompt, vec(g_mem), w_mem_kv[l].astype(BF16))
        yp, kp, vp, cp, fp = _layer(
            yp, mkv_bf[0], mkv_bf[1], None,
            jnp.zeros((bp, CONV_WIDTH - 1, D_MODEL), F32),
            jnp.zeros((bp, FFN_CONV_WIDTH - 1, 2 * FFN_DIM), F32), w)
        ys, ks, vs, cs, fs = _layer(
            ys, cache_mem_k[l], cache_mem_v[l], (cache_sb_k[l], cache_sb_v[l]),
            state_conv[l], state_ffn_conv[l], w)
        for lst, val in zip(outs, (kp, vp, ks, vs, cp, cs, fp, fs, mkv[0], mkv[1])):
            lst.append(val)
    return (yp, ys) + tuple(jnp.stack(o) for o in outs)
```

```python
import functools

import jax
import jax.numpy as jnp
from jax import lax
from jax.experimental import pallas as pl
from jax.experimental.pallas import tpu as pltpu

F32 = jnp.float32
BF16 = jnp.bfloat16

D_MODEL = 1024
SB_HEADS = 16
SB_HEAD_DIM = 64
MEM_HEADS = 4
MEM_HEAD_DIM = 256
MEM_TOKENS = 256
CONV_WIDTH = 31
FFN_DIM = 2816
FFN_CONV_WIDTH = 3
NORM_EPS = 1e-6

SUBLANES = 8
LANES = 128
CONV_HALO = 32
VMEM_LIMIT_BYTES = 56 * 1024 * 1024

SB_DEAD_LOG = -110.0
SB_MASKED_SCORE = -1e4
SB_BLOCK = 256
SB_PROMPT_HEADS = 16
SB_SAMPLE_HEADS = 16
SB_SAMPLE_NEAR = 512


def _rms(x, g):
    return x * lax.rsqrt(jnp.mean(x * x, axis=-1, keepdims=True) + NORM_EPS) * g


def _sigmoid(x):
    return 0.5 * jnp.tanh(0.5 * x) + 0.5


def _softplus(z):
    sign = jnp.uint32(0x80000000)
    neg_abs = lax.bitcast_convert_type(lax.bitcast_convert_type(z, jnp.uint32) | sign, F32)
    return jnp.maximum(z, 0.0) + jnp.log(1.0 + jnp.exp(neg_abs))


def _dot(a, b):
    return jnp.dot(a, b, preferred_element_type=F32)


def _dot_nt(a, b):
    return lax.dot_general(a, b, (((1,), (1,)), ((), ())), preferred_element_type=F32)


def _params(sem):
    return pltpu.CompilerParams(dimension_semantics=sem, vmem_limit_bytes=VMEM_LIMIT_BYTES)


def _resident(block, index_map):
    return pl.BlockSpec(block, index_map, pipeline_mode=pl.Buffered(1))


def _memkv_kernel(mem_ref, g_ref, w_ref, o_ref, ob_ref):
    hb = _rms(mem_ref[0], g_ref[...]).astype(BF16)
    for kv in range(2):
        for h in range(MEM_HEADS):
            c0 = (kv * MEM_HEADS + h) * MEM_HEAD_DIM
            res = _dot(hb, w_ref[:, c0:c0 + MEM_HEAD_DIM])
            o_ref[kv, 0, h] = res
            ob_ref[kv, 0, h] = res.astype(BF16)


def _memkv(mem, g, w_bf):
    b = mem.shape[0]
    block = (2, 1, MEM_HEADS, MEM_TOKENS, MEM_HEAD_DIM)
    out_spec = pl.BlockSpec(block, lambda i: (0, i, 0, 0, 0))
    shape = (2, b, MEM_HEADS, MEM_TOKENS, MEM_HEAD_DIM)
    return pl.pallas_call(
        _memkv_kernel,
        grid=(b,),
        in_specs=[
            pl.BlockSpec((1, MEM_TOKENS, D_MODEL), lambda i: (i, 0, 0)),
            pl.BlockSpec((1, D_MODEL), lambda i: (0, 0)),
            _resident((D_MODEL, 2 * MEM_HEADS * MEM_HEAD_DIM), lambda i: (0, 0)),
        ],
        out_specs=[out_spec, out_spec],
        out_shape=[jax.ShapeDtypeStruct(shape, F32), jax.ShapeDtypeStruct(shape, BF16)],
        compiler_params=_params(("arbitrary",)),
        name="memkv",
    )(mem, g, w_bf)


QKV_COLS = 512


def _qkv_kernel(x_ref, g_ref, w_ref, q_ref, k_ref, v_ref, kf_ref, vf_ref, hb_ref, *, G, tt, dmajor):
    rows = G * tt
    cc = QKV_COLS
    hb_ref[...] = _rms(x_ref[...].reshape(rows, D_MODEL), g_ref[...]).astype(BF16)

    def tok(ref, n0, val):
        ref[:, :, n0:n0 + cc] = val.reshape(G, tt, cc).astype(ref.dtype)

    def heads(ref, n0, res):
        if dmajor:
            ref[0, n0:n0 + cc, :] = res.T
        else:
            for g in range(G):
                for hh in range(cc // SB_HEAD_DIM):
                    ref[g, n0 // SB_HEAD_DIM + hh] = res[g * tt:(g + 1) * tt, hh * SB_HEAD_DIM:(hh + 1) * SB_HEAD_DIM]

    for grp, (tref, href) in enumerate(((q_ref, None), (k_ref, kf_ref), (v_ref, vf_ref))):
        for n0 in range(0, D_MODEL, cc):
            res = _dot(hb_ref[...], w_ref[:, grp * D_MODEL + n0:grp * D_MODEL + n0 + cc])
            tok(tref, n0, res * (SB_HEAD_DIM ** -0.5) if href is None else res)
            if href is not None:
                heads(href, n0, res)


def _qkv(x, g, w_bf, *, G, tt, dmajor):
    b, t, _ = x.shape
    rows = G * tt
    tok_spec = pl.BlockSpec((G, tt, D_MODEL), lambda i, s: (i, s, 0))
    if dmajor:
        head_spec = pl.BlockSpec((G, D_MODEL, tt), lambda i, s: (i, 0, s))
        head_shape = jax.ShapeDtypeStruct((b, D_MODEL, t), F32)
    else:
        head_spec = pl.BlockSpec((G, SB_HEADS, tt, SB_HEAD_DIM), lambda i, s: (i, 0, s, 0))
        head_shape = jax.ShapeDtypeStruct((b, SB_HEADS, t, SB_HEAD_DIM), F32)
    tok_bf = jax.ShapeDtypeStruct((b, t, D_MODEL), BF16)
    return pl.pallas_call(
        functools.partial(_qkv_kernel, G=G, tt=tt, dmajor=dmajor),
        grid=(b // G, t // tt),
        in_specs=[tok_spec,
                  pl.BlockSpec((1, D_MODEL), lambda i, s: (0, 0)),
                  _resident((D_MODEL, 3 * D_MODEL), lambda i, s: (0, 0))],
        out_specs=[tok_spec, tok_spec, tok_spec, head_spec, head_spec],
        out_shape=[tok_bf, tok_bf, tok_bf, head_shape, head_shape],
        scratch_shapes=[pltpu.VMEM((rows, D_MODEL), BF16)],
        compiler_params=_params(("arbitrary", "arbitrary")),
        name="qkv",
    )(x, g, w_bf)


def _neg_upper(n):
    j = lax.broadcasted_iota(jnp.int32, (n, n), 0)
    s = lax.broadcasted_iota(jnp.int32, (n, n), 1)
    return jnp.where(j > s, -1.0, 0.0).astype(BF16)


def _strict_lower(n):
    r = lax.broadcasted_iota(jnp.int32, (n, n), 0)
    s = lax.broadcasted_iota(jnp.int32, (n, n), 1)
    return s < r


def _sb_mask(z, mask):
    return z if mask is None else jnp.where(mask, z, SB_MASKED_SCORE)


def _sb_scores(z, nu, mask):
    z = _sb_mask(z, mask)
    nl = _softplus(z)
    log_beta = z - nl
    excl = _dot(nl.astype(BF16), nu)
    tot = excl[:, 0:1] - nl[:, 0:1]
    return log_beta, excl, tot


def _sb_weights(log_beta, excl, c):
    return jnp.exp(log_beta + excl + c).astype(BF16)


def _sb_prompt_kernel(q_ref, k_ref, v_ref, nu_ref, o_ref, acc_ref, c_ref, *, tq):
    qi = pl.program_id(2)
    nu = nu_ref[...]
    mask = _strict_lower(tq)
    q = q_ref[0]
    hsl = [slice(h * SB_HEAD_DIM, (h + 1) * SB_HEAD_DIM) for h in range(2)]

    def kv(kb):
        start = pl.multiple_of(kb * tq, tq)
        return k_ref[0, pl.ds(start, tq), :], v_ref[0, pl.ds(start, tq), :]

    def cmax():
        return jnp.maximum(jnp.max(c_ref[0]), jnp.max(c_ref[1]))

    @pl.when(qi == 0)
    def _():
        kd, vd = kv(qi)
        for h in range(2):
            lb, ex, tot = _sb_scores(_dot_nt(q[:, hsl[h]], kd[:, hsl[h]]), nu, mask)
            acc_ref[h] = _dot(_sb_weights(lb, ex, 0.0), vd[:, hsl[h]])
            c_ref[h] = tot

    @pl.when(qi > 0)
    def _():
        kd, vd = kv(qi)
        kp, vp = kv(qi - 1)
        for h in range(2):
            lb, ex, tot = _sb_scores(_dot_nt(q[:, hsl[h]], kd[:, hsl[h]]), nu, mask)
            lbp, exp_, totp = _sb_scores(_dot_nt(q[:, hsl[h]], kp[:, hsl[h]]), nu, None)
            o = _dot(_sb_weights(lb, ex, 0.0), vd[:, hsl[h]])
            o = o + _dot(_sb_weights(lbp, exp_, tot), vp[:, hsl[h]])
            acc_ref[h] = o
            c_ref[h] = tot + totp

    def cond(carry):
        kb, cm = carry
        return jnp.logical_and(kb >= 0, cm > SB_DEAD_LOG)

    def body(carry):
        kb, _ = carry
        kk, vv = kv(kb)
        for h in range(2):
            lb, ex, tot = _sb_scores(_dot_nt(q[:, hsl[h]], kk[:, hsl[h]]), nu, None)
            c = c_ref[h]
            acc_ref[h] += _dot(_sb_weights(lb, ex, c), vv[:, hsl[h]])
            c_ref[h] = c + tot
        return kb - 1, cmax()

    lax.while_loop(cond, body, (qi - 2, cmax()))
    o_ref[0] = jnp.concatenate([acc_ref[0], acc_ref[1]], axis=1).astype(BF16)


def _sb_near_kernel(q_ref, kd_ref, kp_ref, vd_ref, vp_ref, nu_ref, o_ref, flag_ref, *, tq, hb):
    qi = pl.program_id(2)
    nu = nu_ref[...]
    mask = _strict_lower(tq)
    q, kd, vd = q_ref[0], kd_ref[0], vd_ref[0]
    hs = [slice(h * SB_HEAD_DIM, (h + 1) * SB_HEAD_DIM) for h in range(hb)]

    @pl.when(qi == 0)
    def _():
        outs = []
        for h in range(hb):
            lb, ex, _ = _sb_scores(_dot_nt(q[:, hs[h]], kd[:, hs[h]]), nu, mask)
            outs.append(_dot(_sb_weights(lb, ex, 0.0), vd[:, hs[h]]))
        o_ref[0] = jnp.concatenate(outs, axis=1).astype(BF16)
        flag_ref[...] = jnp.full(flag_ref.shape, 2.0 * SB_DEAD_LOG, F32)

    @pl.when(qi > 0)
    def _():
        kp, vp = kp_ref[0], vp_ref[0]
        outs = []
        cm = None
        for h in range(hb):
            lb, ex, tot = _sb_scores(_dot_nt(q[:, hs[h]], kd[:, hs[h]]), nu, mask)
            lbp, exp_, totp = _sb_scores(_dot_nt(q[:, hs[h]], kp[:, hs[h]]), nu, None)
            o = _dot(_sb_weights(lb, ex, 0.0), vd[:, hs[h]])
            outs.append(o + _dot(_sb_weights(lbp, exp_, tot), vp[:, hs[h]]))
            m = jnp.max(tot + totp)
            cm = m if cm is None else jnp.maximum(cm, m)
        o_ref[0] = jnp.concatenate(outs, axis=1).astype(BF16)
        flag_ref[...] = jnp.full(flag_ref.shape, jnp.where(qi > 1, cm, 2.0 * SB_DEAD_LOG), F32)


def _sb_prompt(q, k, v, *, tq):
    b, t, _ = q.shape
    hb = SB_PROMPT_HEADS
    ng = SB_HEADS // hb
    nq = t // tq
    lanes = hb * SB_HEAD_DIM
    cur = pl.BlockSpec((1, tq, lanes), lambda i, h, s: (i, s, h))
    prev = pl.BlockSpec((1, tq, lanes), lambda i, h, s: (i, jnp.maximum(s - 1, 0), h))
    o, flag = pl.pallas_call(
        functools.partial(_sb_near_kernel, tq=tq, hb=hb),
        grid=(b, ng, nq),
        in_specs=[cur, cur, prev, cur, prev, pl.BlockSpec((tq, tq), lambda i, h, s: (0, 0))],
        out_specs=[cur, pl.BlockSpec((1, 1, 1, SUBLANES, LANES), lambda i, h, s: (i, h, s, 0, 0))],
        out_shape=[jax.ShapeDtypeStruct((b, t, D_MODEL), BF16),
                   jax.ShapeDtypeStruct((b, ng, nq, SUBLANES, LANES), F32)],
        compiler_params=_params(("arbitrary", "arbitrary", "arbitrary")),
        name="sb_near",
    )(q, k, k, v, v, _neg_upper(tq))
    return lax.cond(jnp.max(flag) > SB_DEAD_LOG, lambda: _sb_prompt_full(q, k, v, tq=tq), lambda: o)


def _sb_prompt_full(q, k, v, *, tq):
    b, t, _ = q.shape
    hp = SB_HEADS // 2
    lanes = 2 * SB_HEAD_DIM
    nu = _neg_upper(tq)
    return pl.pallas_call(
        functools.partial(_sb_prompt_kernel, tq=tq),
        grid=(b, hp, t // tq),
        in_specs=[
            pl.BlockSpec((1, tq, lanes), lambda i, h, s: (i, s, h)),
            pl.BlockSpec((1, t, lanes), lambda i, h, s: (i, 0, h)),
            pl.BlockSpec((1, t, lanes), lambda i, h, s: (i, 0, h)),
            pl.BlockSpec((tq, tq), lambda i, h, s: (0, 0)),
        ],
        out_specs=pl.BlockSpec((1, tq, lanes), lambda i, h, s: (i, s, h)),
        out_shape=jax.ShapeDtypeStruct((b, t, D_MODEL), BF16),
        scratch_shapes=[pltpu.VMEM((2, tq, SB_HEAD_DIM), F32), pltpu.VMEM((2, tq, 1), F32)],
        compiler_params=_params(("arbitrary", "arbitrary", "arbitrary")),
        name="sb_prompt",
    )(q, k, v, nu)


def _sb_sample_kernel(q_ref, kn_ref, vn_ref, kc_ref, vc_ref, nun_ref, nup_ref, acc_ref, c_ref, flag_ref,
                      *, t, hb, nblk, tk):
    mask = _strict_lower(t)
    q, kn, vn = q_ref[0], kn_ref[0], vn_ref[0]
    hs = [slice(h * SB_HEAD_DIM, (h + 1) * SB_HEAD_DIM) for h in range(hb)]
    ks = [slice(kb * tk, (kb + 1) * tk) for kb in range(nblk - 1, -1, -1)]

    lbs, nls = [], []
    for h in range(hb):
        qh = q[:, hs[h]]
        zs = [_sb_mask(_dot_nt(qh, kn[:, hs[h]]), mask)]
        zs += [_dot(qh, kc_ref[0, h, :, s].astype(BF16)) for s in ks]
        nl = [_softplus(z) for z in zs]
        lbs.append([z - n for z, n in zip(zs, nl)])
        nls.append(nl)

    excl = []
    for sg in range(nblk + 1):
        stack = jnp.concatenate([nls[h][sg] for h in range(hb)], axis=0).astype(BF16)
        excl.append(_dot(stack, nun_ref[...] if sg == 0 else nup_ref[...]))

    cm = None
    for h in range(hb):
        rs = slice(h * t, (h + 1) * t)
        ex = excl[0][rs]
        o = _dot(_sb_weights(lbs[h][0], ex, 0.0), vn[:, hs[h]])
        c = ex[:, 0:1] - nls[h][0][:, 0:1]
        for i, s in enumerate(ks):
            ex = excl[i + 1][rs]
            o = o + _dot_nt(_sb_weights(lbs[h][i + 1], ex, c), vc_ref[0, h, :, s].astype(BF16))
            c = c + (ex[:, 0:1] - nls[h][i + 1][:, 0:1])
        acc_ref[0, :, hs[h]] = o
        c_ref[0, h] = jnp.broadcast_to(c, (t, LANES))
        m = jnp.max(c)
        cm = m if cm is None else jnp.maximum(cm, m)
    flag_ref[...] = jnp.full(flag_ref.shape, cm, F32)


def _sb_tail_kernel(q_ref, kc_ref, vc_ref, nu_ref, acc_in_ref, c_in_ref, acc_ref, acc_scr, c_scr):
    s = pl.program_id(2)

    @pl.when(s == 0)
    def _():
        for h in range(2):
            acc_scr[h] = acc_in_ref[0][:, h * SB_HEAD_DIM:(h + 1) * SB_HEAD_DIM]
            c_scr[h] = c_in_ref[0, h][:, 0:1]

    @pl.when(jnp.maximum(jnp.max(c_scr[0]), jnp.max(c_scr[1])) > SB_DEAD_LOG)
    def _():
        for h in range(2):
            qh = q_ref[0][:, h * SB_HEAD_DIM:(h + 1) * SB_HEAD_DIM]
            lb, ex, tot = _sb_scores(_dot(qh, kc_ref[0, h].astype(BF16)), nu_ref[...], None)
            c = c_scr[h]
            acc_scr[h] += _dot_nt(_sb_weights(lb, ex, c), vc_ref[0, h].astype(BF16))
            c_scr[h] = c + tot

    @pl.when(s == pl.num_programs(2) - 1)
    def _():
        acc_ref[0] = jnp.concatenate([acc_scr[0], acc_scr[1]], axis=1)


def _sb_sample(q, kn, vn, kct, vct):
    b, t, _ = q.shape
    past = kct.shape[3]
    tk = SB_BLOCK
    near = min(SB_SAMPLE_NEAR, past)
    hb = SB_SAMPLE_HEADS
    ng = SB_HEADS // hb
    new_spec = pl.BlockSpec((1, t, hb * SB_HEAD_DIM), lambda i, h: (i, 0, h))
    cache_spec = pl.BlockSpec((1, hb, SB_HEAD_DIM, near), lambda i, h: (i, h, 0, past // near - 1))
    nun, nup = _neg_upper(t), _neg_upper(tk)
    acc, c, flag = pl.pallas_call(
        functools.partial(_sb_sample_kernel, t=t, hb=hb, nblk=near // tk, tk=tk),
        grid=(b, ng),
        in_specs=[new_spec, new_spec, new_spec, cache_spec, cache_spec,
                  pl.BlockSpec((t, t), lambda i, h: (0, 0)),
                  pl.BlockSpec((tk, tk), lambda i, h: (0, 0))],
        out_specs=[new_spec,
                   pl.BlockSpec((1, hb, t, LANES), lambda i, h: (i, h, 0, 0)),
                   pl.BlockSpec((1, 1, SUBLANES, LANES), lambda i, h: (i, h, 0, 0))],
        out_shape=[jax.ShapeDtypeStruct((b, t, D_MODEL), F32),
                   jax.ShapeDtypeStruct((b, SB_HEADS, t, LANES), F32),
                   jax.ShapeDtypeStruct((b, ng, SUBLANES, LANES), F32)],
        compiler_params=_params(("arbitrary", "arbitrary")),
        name="sb_sample",
    )(q, kn, vn, kct, vct, nun, nup)
    nfar = (past - near) // tk
    if nfar == 0:
        return acc

    def tail():
        pair = pl.BlockSpec((1, t, 2 * SB_HEAD_DIM), lambda i, h, s: (i, 0, h))
        far = pl.BlockSpec((1, 2, SB_HEAD_DIM, tk), lambda i, h, s: (i, h, 0, nfar - 1 - s))
        return pl.pallas_call(
            _sb_tail_kernel,
            grid=(b, SB_HEADS // 2, nfar),
            in_specs=[pair, far, far, pl.BlockSpec((tk, tk), lambda i, h, s: (0, 0)), pair,
                      pl.BlockSpec((1, 2, t, LANES), lambda i, h, s: (i, h, 0, 0))],
            out_specs=pair,
            out_shape=jax.ShapeDtypeStruct((b, t, D_MODEL), F32),
            scratch_shapes=[pltpu.VMEM((2, t, SB_HEAD_DIM), F32), pltpu.VMEM((2, t, 1), F32)],
            compiler_params=_params(("arbitrary", "arbitrary", "arbitrary")),
            name="sb_tail",
        )(q, kct, vct, nup, acc, c)

    return lax.cond(jnp.max(flag) > SB_DEAD_LOG, tail, lambda: acc)


CONV_ROWS = 128
CONV_LANES = 128


def _mix_kernel(x_ref, osb_ref, left_ref, mk_ref, mv_ref, gpre_ref, wa_ref, wb_ref,
                wsb_ref, wcv_ref, wmo_ref, wout_ref, dww_ref, dwb_ref, lng_ref, lnb_ref, gpost_ref,
                y_ref, ulast_ref, hb_ref, uf_ref, cc_ref, qm_ref, om_ref, mg_ref, gb_ref, *, G, tt):
    s = pl.program_id(1)
    rows = G * tt
    H0 = CONV_HALO
    off = H0 - (CONV_WIDTH - 1)
    x = x_ref[...].reshape(rows, D_MODEL)
    hb_ref[...] = _rms(x, gpre_ref[...]).astype(BF16)

    def proj(w_ref, grp):
        return _dot(hb_ref[...], w_ref[:, grp * D_MODEL:(grp + 1) * D_MODEL])

    @pl.when(s == 0)
    def _():
        uf_ref[:, 0:H0, :] = left_ref[...]

    @pl.when(s > 0)
    def _():
        uf_ref[:, 0:H0, :] = uf_ref[:, tt:tt + H0, :]

    uf_ref[:, H0:H0 + tt, :] = (proj(wa_ref, 0) * _sigmoid(proj(wa_ref, 1))).reshape(G, tt, D_MODEL)
    ulast_ref[...] = uf_ref[:, tt:tt + H0, :]

    qm_ref[...] = (proj(wa_ref, 2) * (MEM_HEAD_DIM ** -0.5)).astype(BF16)
    for g in range(G):
        for h in range(MEM_HEADS):
            cs = slice(h * MEM_HEAD_DIM, (h + 1) * MEM_HEAD_DIM)
            sc = _dot_nt(qm_ref[g * tt:(g + 1) * tt, cs], mk_ref[g, h].astype(BF16))
            p = jnp.exp(sc - jnp.max(sc, axis=-1, keepdims=True))
            den = jnp.sum(p, axis=-1, keepdims=True)
            oh = _dot(p.astype(BF16), mv_ref[g, h].astype(BF16)) * (1.0 / den)
            om_ref[g * tt:(g + 1) * tt, cs] = oh.astype(BF16)
    y_sb = _dot(osb_ref[...].reshape(rows, D_MODEL).astype(BF16), wsb_ref[...])
    mg_ref[...] = _sigmoid(proj(wb_ref, 0)) * y_sb
    mg_ref[...] += _sigmoid(proj(wb_ref, 2)) * _dot(om_ref[...], wmo_ref[...])
    gb_ref[...] = _sigmoid(proj(wb_ref, 1))

    rc = min(CONV_ROWS, tt)
    win = rc + H0
    for g in range(G):
        for r0 in range(0, tt, rc):
            for c0 in range(0, D_MODEL, CONV_LANES):
                cs = slice(c0, c0 + CONV_LANES)
                window = uf_ref[g, r0:r0 + win, cs]
                acc = jnp.broadcast_to(dwb_ref[:, cs], (rc, CONV_LANES))
                for r in range(SUBLANES):
                    shifted = window if r == 0 else pltpu.roll(window, win - r, axis=0)
                    for m in range(H0 // SUBLANES + 1):
                        j = SUBLANES * m + r - off
                        if 0 <= j < CONV_WIDTH:
                            acc = acc + shifted[SUBLANES * m:SUBLANES * m + rc] * dww_ref[j:j + 1, cs]
                cc_ref[g * tt + r0:g * tt + r0 + rc, cs] = acc

    cc = cc_ref[...]
    mu = jnp.mean(cc, axis=-1, keepdims=True)
    d = cc - mu
    var = jnp.mean(d * d, axis=-1, keepdims=True)
    yn = d * lax.rsqrt(var + NORM_EPS) * lng_ref[...] + lnb_ref[...]
    y_conv = _dot((yn * _sigmoid(yn)).astype(BF16), wcv_ref[...])

    merged = mg_ref[...] + gb_ref[...] * y_conv
    out = _dot(merged.astype(BF16), wout_ref[...])
    y_ref[...] = (x + _rms(out, gpost_ref[...])).reshape(G, tt, D_MODEL)


def _mix(x, osb, left, mk, mv, gpre, w_in, wsb, wcv, wmo, wout, dww, dwb, lng, lnb, gpost, *, G, tt):
    b, t, _ = x.shape
    rows = G * tt
    tok = pl.BlockSpec((G, tt, D_MODEL), lambda i, s: (i, s, 0))
    halo = pl.BlockSpec((G, CONV_HALO, D_MODEL), lambda i, s: (i, 0, 0))
    mem = pl.BlockSpec((G, MEM_HEADS, MEM_TOKENS, MEM_HEAD_DIM), lambda i, s: (i, 0, 0, 0))
    wsq = _resident((D_MODEL, D_MODEL), lambda i, s: (0, 0))
    vec = pl.BlockSpec((1, D_MODEL), lambda i, s: (0, 0))
    return pl.pallas_call(
        functools.partial(_mix_kernel, G=G, tt=tt),
        grid=(b // G, t // tt),
        in_specs=[tok, tok, halo, mem, mem, vec,
                  _resident((D_MODEL, 3 * D_MODEL), lambda i, s: (0, 1)),
                  _resident((D_MODEL, 3 * D_MODEL), lambda i, s: (0, 2)),
                  wsq, wsq, wsq, wsq,
                  pl.BlockSpec((CONV_WIDTH, D_MODEL), lambda i, s: (0, 0)),
                  vec, vec, vec, vec],
        out_specs=[tok, halo],
        out_shape=[jax.ShapeDtypeStruct((b, t, D_MODEL), F32),
                   jax.ShapeDtypeStruct((b, CONV_HALO, D_MODEL), F32)],
        scratch_shapes=[pltpu.VMEM((rows, D_MODEL), BF16),
                        pltpu.VMEM((G, CONV_HALO + tt, D_MODEL), F32),
                        pltpu.VMEM((rows, D_MODEL), F32),
                        pltpu.VMEM((rows, D_MODEL), BF16),
                        pltpu.VMEM((rows, D_MODEL), BF16),
                        pltpu.VMEM((rows, D_MODEL), F32),
                        pltpu.VMEM((rows, D_MODEL), F32)],
        compiler_params=_params(("arbitrary", "arbitrary")),
        name="mix",
    )(x, osb, left, mk, mv, gpre, w_in, w_in, wsb, wcv, wmo, wout, dww, dwb, lng, lnb, gpost)


FFN_ROWS = 256
FFN_COLS = 512


def _gelu_tanh(x):
    return 0.5 * x * (1.0 + jnp.tanh(0.7978845608028654 * (x + 0.044715 * (x * x * x))))


def _ffn_kernel(x_ref, lg_ref, lv_ref, gpre_ref, gpost_ref, wg_ref, wv_ref, dwg_ref, dwv_ref, wd_ref,
                y_ref, st_ref, hb_ref, acc_ref, upf_ref, carry_ref, *, G, tt, nf):
    s = pl.program_id(1)
    f = pl.program_id(2)
    rows = G * tt
    S = SUBLANES
    lefts = (lg_ref, lv_ref)
    ws = (wg_ref, wv_ref)
    dws = (dwg_ref, dwv_ref)

    @pl.when(f == 0)
    def _():
        hb_ref[...] = _rms(x_ref[...].reshape(rows, D_MODEL), gpre_ref[...]).astype(BF16)
        acc_ref[...] = jnp.zeros_like(acc_ref)

    for half in range(2):
        @pl.when(s == 0)
        def _():
            upf_ref[half, :, 0:S, :] = lefts[half][...]

        @pl.when(s > 0)
        def _():
            upf_ref[half, :, 0:S, :] = carry_ref[f, half]

    rc = min(FFN_ROWS, rows)
    gpc = max(rc // tt, 1)
    tc = min(rc, tt)
    tf = wd_ref.shape[0]
    for r0 in range(0, rows, rc):
        g0, t0 = r0 // tt, r0 % tt
        hb = hb_ref[r0:r0 + rc, :]
        acts = []
        for c0 in range(0, tf, FFN_COLS):
            cs = slice(c0, min(c0 + FFN_COLS, tf))
            conv = []
            for half in range(2):
                up = _dot(hb, ws[half][:, cs]).reshape(gpc, tc, -1)
                upf_ref[half, g0:g0 + gpc, S + t0:S + t0 + tc, cs] = up
                dw = dws[half]
                c = (up * dw[2:3, cs]
                     + upf_ref[half, g0:g0 + gpc, S + t0 - 1:S + t0 - 1 + tc, cs] * dw[1:2, cs]
                     + upf_ref[half, g0:g0 + gpc, S + t0 - 2:S + t0 - 2 + tc, cs] * dw[0:1, cs])
                conv.append(c.reshape(rc, -1))
            acts.append((_gelu_tanh(conv[0]) * conv[1]).astype(BF16))
        acc_ref[r0:r0 + rc, :] += _dot(jnp.concatenate(acts, axis=1), wd_ref[...])

    for half in range(2):
        last = upf_ref[half, :, tt:tt + S, :]
        carry_ref[f, half] = last
        st_ref[half, f] = last

    @pl.when(f == nf - 1)
    def _():
        y = x_ref[...].reshape(rows, D_MODEL) + _rms(acc_ref[...], gpost_ref[...])
        y_ref[...] = y.reshape(G, tt, D_MODEL)


def _ffn(x, left, gpre, gpost, wup, dw, wdown, *, G, tt, nf):
    b, t, _ = x.shape
    rows = G * tt
    tf = FFN_DIM // nf
    S = SUBLANES
    tok = pl.BlockSpec((G, tt, D_MODEL), lambda i, s, f: (i, s, 0))
    vec = pl.BlockSpec((1, D_MODEL), lambda i, s, f: (0, 0))
    wspec = _resident if nf == 1 else pl.BlockSpec
    y, st = pl.pallas_call(
        functools.partial(_ffn_kernel, G=G, tt=tt, nf=nf),
        grid=(b // G, t // tt, nf),
        in_specs=[tok,
                  pl.BlockSpec((G, S, tf), lambda i, s, f: (i, 0, f)),
                  pl.BlockSpec((G, S, tf), lambda i, s, f: (i, 0, nf + f)),
                  vec, vec,
                  wspec((D_MODEL, tf), lambda i, s, f: (0, f)),
                  wspec((D_MODEL, tf), lambda i, s, f: (0, nf + f)),
                  pl.BlockSpec((FFN_CONV_WIDTH, tf), lambda i, s, f: (0, f)),
                  pl.BlockSpec((FFN_CONV_WIDTH, tf), lambda i, s, f: (0, nf + f)),
                  wspec((tf, D_MODEL), lambda i, s, f: (f, 0))],
        out_specs=[tok, pl.BlockSpec((2, nf, G, S, tf), lambda i, s, f: (0, 0, i, 0, 0))],
        out_shape=[jax.ShapeDtypeStruct((b, t, D_MODEL), F32),
                   jax.ShapeDtypeStruct((2, nf, b, S, tf), F32)],
        scratch_shapes=[pltpu.VMEM((rows, D_MODEL), BF16),
                        pltpu.VMEM((rows, D_MODEL), F32),
                        pltpu.VMEM((2, G, S + tt, tf), F32),
                        pltpu.VMEM((nf, 2, G, S, tf), F32)],
        compiler_params=_params(("arbitrary", "arbitrary", "arbitrary")),
        name="ffn",
    )(x, left, left, gpre, gpost, wup, wup, dw, dw, wdown)
    keep = FFN_CONV_WIDTH - 1
    state = st[:, :, :, S - keep:, :].transpose(2, 3, 0, 1, 4).reshape(b, keep, 2 * FFN_DIM)
    return y, state


def _tiles(b, t):
    if t >= 512:
        return {"qkv": (1, 1024), "mix": (1, 512), "ffn": (1, 512)}
    return {"qkv": (min(b, 512 // t), t), "mix": (min(b, 256 // t), t), "ffn": (min(b, 512 // t), t)}


def _layer(x, mk, mv, sb_cache, conv_left, ffn_left, w):
    b, t, _ = x.shape
    tiles = _tiles(b, t)
    prompt = sb_cache is None
    G, tt = tiles["qkv"]
    q, k, v, kf, vf = _qkv(x, w["g_mix_pre"], w["w_in"], G=G, tt=tt, dmajor=prompt)
    if prompt:
        osb = _sb_prompt(q, k, v, tq=SB_BLOCK)
        kf = kf.reshape(b, SB_HEADS, SB_HEAD_DIM, t).swapaxes(2, 3)
        vf = vf.reshape(b, SB_HEADS, SB_HEAD_DIM, t).swapaxes(2, 3)
    else:
        osb = _sb_sample(q, k, v, sb_cache[0].swapaxes(2, 3), sb_cache[1].swapaxes(2, 3))

    keep = CONV_WIDTH - 1
    left = jnp.pad(conv_left, ((0, 0), (CONV_HALO - keep, 0), (0, 0)))
    G, tt = tiles["mix"]
    x1, ulast = _mix(x, osb, left, mk, mv, w["g_mix_pre"], w["w_in"], w["w_sb_o"], w["w_conv_o"], w["w_mem_o"],
                     w["w_out"], w["conv_dw_w"], w["conv_dw_b"], w["conv_ln_g"], w["conv_ln_b"], w["g_mix_post"],
                     G=G, tt=tt)
    conv_state = ulast[:, CONV_HALO - keep:]

    fkeep = FFN_CONV_WIDTH - 1
    fleft = jnp.pad(ffn_left, ((0, 0), (SUBLANES - fkeep, 0), (0, 0)))
    G, tt = tiles["ffn"]
    y, ffn_state = _ffn(x1, fleft, w["g_ffn_pre"], w["g_ffn_post"], w["w_ffn_up"], w["ffn_dw_w"],
                        w["w_ffn_down"], G=G, tt=tt, nf=1)
    return y, kf, vf, conv_state, ffn_state


def kernel(x_prompt, x_sample, mem_prompt, cache_sb_k, cache_sb_v, state_conv, state_ffn_conv, cache_mem_k, cache_mem_v, g_mem, w_mem_kv, g_mix_pre, g_mix_post, w_in, w_sb_o, conv_dw_w, conv_dw_b, conv_ln_g, conv_ln_b, w_conv_o, w_mem_o, w_out, g_ffn_pre, g_ffn_post, w_ffn_up, ffn_dw_w, w_ffn_down):
    depth = w_in.shape[0]
    bp = x_prompt.shape[0]
    yp, ys = x_prompt, x_sample
    outs = [[] for _ in range(10)]
    for l in range(depth):
        vec = lambda a: a[l].reshape(1, -1)
        w = {
            "g_mix_pre": vec(g_mix_pre), "g_mix_post": vec(g_mix_post),
            "w_in": w_in[l].astype(BF16), "w_sb_o": w_sb_o[l].astype(BF16),
            "conv_dw_w": conv_dw_w[l], "conv_dw_b": vec(conv_dw_b),
            "conv_ln_g": vec(conv_ln_g), "conv_ln_b": vec(conv_ln_b),
            "w_conv_o": w_conv_o[l].astype(BF16), "w_mem_o": w_mem_o[l].astype(BF16),
            "w_out": w_out[l].astype(BF16),
            "g_ffn_pre": vec(g_ffn_pre), "g_ffn_post": vec(g_ffn_post),
            "w_ffn_up": w_ffn_up[l].astype(BF16), "ffn_dw_w": ffn_dw_w[l],
            "w_ffn_down": w_ffn_down[l].astype(BF16),
        }
        mkv, mkv_bf = _memkv(mem_prompt, vec(g_mem), w_mem_kv[l].astype(BF16))
        yp, kp, vp, cp, fp = _layer(
            yp, mkv_bf[0], mkv_bf[1], None,
            jnp.zeros((bp, CONV_WIDTH - 1, D_MODEL), F32),
            jnp.zeros((bp, FFN_CONV_WIDTH - 1, 2 * FFN_DIM), F32), w)
        ys, ks, vs, cs, fs = _layer(
            ys, cache_mem_k[l], cache_mem_v[l], (cache_sb_k[l], cache_sb_v[l]),
            state_conv[l], state_ffn_conv[l], w)
        for lst, val in zip(outs, (kp, vp, ks, vs, cp, cs, fp, fs, mkv[0], mkv[1])):
            lst.append(val)
    return (yp, ys) + tuple(jnp.stack(o) for o in outs)
```

```python
import functools

import jax
import jax.numpy as jnp
from jax import lax
from jax.experimental import pallas as pl
from jax.experimental.pallas import tpu as pltpu

F32 = jnp.float32
BF16 = jnp.bfloat16

D_MODEL = 1024
SB_HEADS = 16
SB_HEAD_DIM = 64
MEM_HEADS = 4
MEM_HEAD_DIM = 256
MEM_TOKENS = 256
CONV_WIDTH = 31
FFN_DIM = 2816
FFN_CONV_WIDTH = 3
NORM_EPS = 1e-6

SUBLANES = 8
LANES = 128
CONV_HALO = 32
VMEM_LIMIT_BYTES = 56 * 1024 * 1024

SB_DEAD_LOG = -110.0
SB_MASKED_SCORE = -1e4
SB_BLOCK = 256
SB_PROMPT_HEADS = 16
SB_SAMPLE_HEADS = 16
SB_SAMPLE_NEAR = 256


def _rms(x, g):
    return x * lax.rsqrt(jnp.mean(x * x, axis=-1, keepdims=True) + NORM_EPS) * g


def _sigmoid(x):
    return 0.5 * jnp.tanh(0.5 * x) + 0.5


def _softplus(z):
    sign = jnp.uint32(0x80000000)
    neg_abs = lax.bitcast_convert_type(lax.bitcast_convert_type(z, jnp.uint32) | sign, F32)
    return jnp.maximum(z, 0.0) + jnp.log(1.0 + jnp.exp(neg_abs))


def _dot(a, b):
    return jnp.dot(a, b, preferred_element_type=F32)


def _dot_nt(a, b):
    return lax.dot_general(a, b, (((1,), (1,)), ((), ())), preferred_element_type=F32)


def _params(sem):
    return pltpu.CompilerParams(dimension_semantics=sem, vmem_limit_bytes=VMEM_LIMIT_BYTES)


def _resident(block, index_map):
    return pl.BlockSpec(block, index_map, pipeline_mode=pl.Buffered(1))


def _memkv_kernel(mem_ref, g_ref, w_ref, o_ref, ob_ref):
    hb = _rms(mem_ref[0], g_ref[...]).astype(BF16)
    for kv in range(2):
        for h in range(MEM_HEADS):
            c0 = (kv * MEM_HEADS + h) * MEM_HEAD_DIM
            res = _dot(hb, w_ref[:, c0:c0 + MEM_HEAD_DIM])
            o_ref[kv, 0, h] = res
            ob_ref[kv, 0, h] = res.astype(BF16)


def _memkv(mem, g, w_bf):
    b = mem.shape[0]
    block = (2, 1, MEM_HEADS, MEM_TOKENS, MEM_HEAD_DIM)
    out_spec = pl.BlockSpec(block, lambda i: (0, i, 0, 0, 0))
    shape = (2, b, MEM_HEADS, MEM_TOKENS, MEM_HEAD_DIM)
    return pl.pallas_call(
        _memkv_kernel,
        grid=(b,),
        in_specs=[
            pl.BlockSpec((1, MEM_TOKENS, D_MODEL), lambda i: (i, 0, 0)),
            pl.BlockSpec((1, D_MODEL), lambda i: (0, 0)),
            _resident((D_MODEL, 2 * MEM_HEADS * MEM_HEAD_DIM), lambda i: (0, 0)),
        ],
        out_specs=[out_spec, out_spec],
        out_shape=[jax.ShapeDtypeStruct(shape, F32), jax.ShapeDtypeStruct(shape, BF16)],
        compiler_params=_params(("arbitrary",)),
        name="memkv",
    )(mem, g, w_bf)


QKV_COLS = 512


def _qkv_kernel(x_ref, g_ref, w_ref, q_ref, k_ref, v_ref, kf_ref, vf_ref, hb_ref, *, G, tt, dmajor):
    rows = G * tt
    cc = QKV_COLS
    hb_ref[...] = _rms(x_ref[...].reshape(rows, D_MODEL), g_ref[...]).astype(BF16)

    def tok(ref, n0, val):
        ref[:, :, n0:n0 + cc] = val.reshape(G, tt, cc).astype(ref.dtype)

    def heads(ref, n0, res):
        if dmajor:
            ref[0, n0:n0 + cc, :] = res.T
        else:
            for g in range(G):
                for hh in range(cc // SB_HEAD_DIM):
                    ref[g, n0 // SB_HEAD_DIM + hh] = res[g * tt:(g + 1) * tt, hh * SB_HEAD_DIM:(hh + 1) * SB_HEAD_DIM]

    for grp, (tref, href) in enumerate(((q_ref, None), (k_ref, kf_ref), (v_ref, vf_ref))):
        for n0 in range(0, D_MODEL, cc):
            res = _dot(hb_ref[...], w_ref[:, grp * D_MODEL + n0:grp * D_MODEL + n0 + cc])
            tok(tref, n0, res * (SB_HEAD_DIM ** -0.5) if href is None else res)
            if href is not None:
                heads(href, n0, res)


def _qkv(x, g, w_bf, *, G, tt, dmajor):
    b, t, _ = x.shape
    rows = G * tt
    tok_spec = pl.BlockSpec((G, tt, D_MODEL), lambda i, s: (i, s, 0))
    if dmajor:
        head_spec = pl.BlockSpec((G, D_MODEL, tt), lambda i, s: (i, 0, s))
        head_shape = jax.ShapeDtypeStruct((b, D_MODEL, t), F32)
    else:
        head_spec = pl.BlockSpec((G, SB_HEADS, tt, SB_HEAD_DIM), lambda i, s: (i, 0, s, 0))
        head_shape = jax.ShapeDtypeStruct((b, SB_HEADS, t, SB_HEAD_DIM), F32)
    tok_bf = jax.ShapeDtypeStruct((b, t, D_MODEL), BF16)
    return pl.pallas_call(
        functools.partial(_qkv_kernel, G=G, tt=tt, dmajor=dmajor),
        grid=(b // G, t // tt),
        in_specs=[tok_spec,
                  pl.BlockSpec((1, D_MODEL), lambda i, s: (0, 0)),
                  _resident((D_MODEL, 3 * D_MODEL), lambda i, s: (0, 0))],
        out_specs=[tok_spec, tok_spec, tok_spec, head_spec, head_spec],
        out_shape=[tok_bf, tok_bf, tok_bf, head_shape, head_shape],
        scratch_shapes=[pltpu.VMEM((rows, D_MODEL), BF16)],
        compiler_params=_params(("arbitrary", "arbitrary")),
        name="qkv",
    )(x, g, w_bf)


def _neg_upper(n):
    j = lax.broadcasted_iota(jnp.int32, (n, n), 0)
    s = lax.broadcasted_iota(jnp.int32, (n, n), 1)
    return jnp.where(j > s, -1.0, 0.0).astype(BF16)


def _strict_lower(n):
    r = lax.broadcasted_iota(jnp.int32, (n, n), 0)
    s = lax.broadcasted_iota(jnp.int32, (n, n), 1)
    return s < r


def _sb_mask(z, mask):
    return z if mask is None else jnp.where(mask, z, SB_MASKED_SCORE)


def _sb_scores(z, nu, mask):
    z = _sb_mask(z, mask)
    nl = _softplus(z)
    log_beta = z - nl
    excl = _dot(nl.astype(BF16), nu)
    tot = excl[:, 0:1] - nl[:, 0:1]
    return log_beta, excl, tot


def _sb_weights(log_beta, excl, c):
    return jnp.exp(log_beta + excl + c).astype(BF16)


def _sb_prompt_kernel(q_ref, k_ref, v_ref, nu_ref, o_ref, acc_ref, c_ref, *, tq):
    qi = pl.program_id(2)
    nu = nu_ref[...]
    mask = _strict_lower(tq)
    q = q_ref[0]
    hsl = [slice(h * SB_HEAD_DIM, (h + 1) * SB_HEAD_DIM) for h in range(2)]

    def kv(kb):
        start = pl.multiple_of(kb * tq, tq)
        return k_ref[0, pl.ds(start, tq), :], v_ref[0, pl.ds(start, tq), :]

    def cmax():
        return jnp.maximum(jnp.max(c_ref[0]), jnp.max(c_ref[1]))

    @pl.when(qi == 0)
    def _():
        kd, vd = kv(qi)
        for h in range(2):
            lb, ex, tot = _sb_scores(_dot_nt(q[:, hsl[h]], kd[:, hsl[h]]), nu, mask)
            acc_ref[h] = _dot(_sb_weights(lb, ex, 0.0), vd[:, hsl[h]])
            c_ref[h] = tot

    @pl.when(qi > 0)
    def _():
        kd, vd = kv(qi)
        kp, vp = kv(qi - 1)
        for h in range(2):
            lb, ex, tot = _sb_scores(_dot_nt(q[:, hsl[h]], kd[:, hsl[h]]), nu, mask)
            lbp, exp_, totp = _sb_scores(_dot_nt(q[:, hsl[h]], kp[:, hsl[h]]), nu, None)
            o = _dot(_sb_weights(lb, ex, 0.0), vd[:, hsl[h]])
            o = o + _dot(_sb_weights(lbp, exp_, tot), vp[:, hsl[h]])
            acc_ref[h] = o
            c_ref[h] = tot + totp

    def cond(carry):
        kb, cm = carry
        return jnp.logical_and(kb >= 0, cm > SB_DEAD_LOG)

    def body(carry):
        kb, _ = carry
        kk, vv = kv(kb)
        for h in range(2):
            lb, ex, tot = _sb_scores(_dot_nt(q[:, hsl[h]], kk[:, hsl[h]]), nu, None)
            c = c_ref[h]
            acc_ref[h] += _dot(_sb_weights(lb, ex, c), vv[:, hsl[h]])
            c_ref[h] = c + tot
        return kb - 1, cmax()

    lax.while_loop(cond, body, (qi - 2, cmax()))
    o_ref[0] = jnp.concatenate([acc_ref[0], acc_ref[1]], axis=1).astype(BF16)


def _sb_near_kernel(q_ref, kd_ref, kp_ref, vd_ref, vp_ref, nu_ref, o_ref, flag_ref, *, tq, hb):
    qi = pl.program_id(2)
    nu = nu_ref[...]
    mask = _strict_lower(tq)
    q, kd, vd = q_ref[0], kd_ref[0], vd_ref[0]
    hs = [slice(h * SB_HEAD_DIM, (h + 1) * SB_HEAD_DIM) for h in range(hb)]

    @pl.when(qi == 0)
    def _():
        outs = []
        for h in range(hb):
            lb, ex, _ = _sb_scores(_dot_nt(q[:, hs[h]], kd[:, hs[h]]), nu, mask)
            outs.append(_dot(_sb_weights(lb, ex, 0.0), vd[:, hs[h]]))
        o_ref[0] = jnp.concatenate(outs, axis=1).astype(BF16)
        flag_ref[...] = jnp.full(flag_ref.shape, 2.0 * SB_DEAD_LOG, F32)

    @pl.when(qi > 0)
    def _():
        kp, vp = kp_ref[0], vp_ref[0]
        outs = []
        cm = None
        for h in range(hb):
            lb, ex, tot = _sb_scores(_dot_nt(q[:, hs[h]], kd[:, hs[h]]), nu, mask)
            lbp, exp_, totp = _sb_scores(_dot_nt(q[:, hs[h]], kp[:, hs[h]]), nu, None)
            o = _dot(_sb_weights(lb, ex, 0.0), vd[:, hs[h]])
            outs.append(o + _dot(_sb_weights(lbp, exp_, tot), vp[:, hs[h]]))
            m = jnp.max(tot + totp)
            cm = m if cm is None else jnp.maximum(cm, m)
        o_ref[0] = jnp.concatenate(outs, axis=1).astype(BF16)
        flag_ref[...] = jnp.full(flag_ref.shape, jnp.where(qi > 1, cm, 2.0 * SB_DEAD_LOG), F32)


def _sb_prompt(q, k, v, *, tq):
    b, t, _ = q.shape
    hb = SB_PROMPT_HEADS
    ng = SB_HEADS // hb
    nq = t // tq
    lanes = hb * SB_HEAD_DIM
    cur = pl.BlockSpec((1, tq, lanes), lambda i, h, s: (i, s, h))
    prev = pl.BlockSpec((1, tq, lanes), lambda i, h, s: (i, jnp.maximum(s - 1, 0), h))
    o, flag = pl.pallas_call(
        functools.partial(_sb_near_kernel, tq=tq, hb=hb),
        grid=(b, ng, nq),
        in_specs=[cur, cur, prev, cur, prev, pl.BlockSpec((tq, tq), lambda i, h, s: (0, 0))],
        out_specs=[cur, pl.BlockSpec((1, 1, 1, SUBLANES, LANES), lambda i, h, s: (i, h, s, 0, 0))],
        out_shape=[jax.ShapeDtypeStruct((b, t, D_MODEL), BF16),
                   jax.ShapeDtypeStruct((b, ng, nq, SUBLANES, LANES), F32)],
        compiler_params=_params(("arbitrary", "arbitrary", "arbitrary")),
        name="sb_near",
    )(q, k, k, v, v, _neg_upper(tq))
    return lax.cond(jnp.max(flag) > SB_DEAD_LOG, lambda: _sb_prompt_full(q, k, v, tq=tq), lambda: o)


def _sb_prompt_full(q, k, v, *, tq):
    b, t, _ = q.shape
    hp = SB_HEADS // 2
    lanes = 2 * SB_HEAD_DIM
    nu = _neg_upper(tq)
    return pl.pallas_call(
        functools.partial(_sb_prompt_kernel, tq=tq),
        grid=(b, hp, t // tq),
        in_specs=[
            pl.BlockSpec((1, tq, lanes), lambda i, h, s: (i, s, h)),
            pl.BlockSpec((1, t, lanes), lambda i, h, s: (i, 0, h)),
            pl.BlockSpec((1, t, lanes), lambda i, h, s: (i, 0, h)),
            pl.BlockSpec((tq, tq), lambda i, h, s: (0, 0)),
        ],
        out_specs=pl.BlockSpec((1, tq, lanes), lambda i, h, s: (i, s, h)),
        out_shape=jax.ShapeDtypeStruct((b, t, D_MODEL), BF16),
        scratch_shapes=[pltpu.VMEM((2, tq, SB_HEAD_DIM), F32), pltpu.VMEM((2, tq, 1), F32)],
        compiler_params=_params(("arbitrary", "arbitrary", "arbitrary")),
        name="sb_prompt",
    )(q, k, v, nu)


def _sb_sample_kernel(q_ref, kn_ref, vn_ref, kc_ref, vc_ref, nun_ref, nup_ref, acc_ref, c_ref, flag_ref,
                      *, t, hb, nblk, tk):
    mask = _strict_lower(t)
    q, kn, vn = q_ref[0], kn_ref[0], vn_ref[0]
    hs = [slice(h * SB_HEAD_DIM, (h + 1) * SB_HEAD_DIM) for h in range(hb)]
    ks = [slice(kb * tk, (kb + 1) * tk) for kb in range(nblk - 1, -1, -1)]

    lbs, nls = [], []
    for h in range(hb):
        qh = q[:, hs[h]]
        zs = [_sb_mask(_dot_nt(qh, kn[:, hs[h]]), mask)]
        zs += [_dot(qh, kc_ref[0, h, :, s].astype(BF16)) for s in ks]
        nl = [_softplus(z) for z in zs]
        lbs.append([z - n for z, n in zip(zs, nl)])
        nls.append(nl)

    excl = []
    for sg in range(nblk + 1):
        stack = jnp.concatenate([nls[h][sg] for h in range(hb)], axis=0).astype(BF16)
        excl.append(_dot(stack, nun_ref[...] if sg == 0 else nup_ref[...]))

    cm = None
    for h in range(hb):
        rs = slice(h * t, (h + 1) * t)
        ex = excl[0][rs]
        o = _dot(_sb_weights(lbs[h][0], ex, 0.0), vn[:, hs[h]])
        c = ex[:, 0:1] - nls[h][0][:, 0:1]
        for i, s in enumerate(ks):
            ex = excl[i + 1][rs]
            o = o + _dot_nt(_sb_weights(lbs[h][i + 1], ex, c), vc_ref[0, h, :, s].astype(BF16))
            c = c + (ex[:, 0:1] - nls[h][i + 1][:, 0:1])
        acc_ref[0, :, hs[h]] = o
        c_ref[0, h] = jnp.broadcast_to(c, (t, LANES))
        m = jnp.max(c)
        cm = m if cm is None else jnp.maximum(cm, m)
    flag_ref[...] = jnp.full(flag_ref.shape, cm, F32)


def _sb_tail_kernel(q_ref, kc_ref, vc_ref, nu_ref, acc_in_ref, c_in_ref, acc_ref, acc_scr, c_scr):
    s = pl.program_id(2)

    @pl.when(s == 0)
    def _():
        for h in range(2):
            acc_scr[h] = acc_in_ref[0][:, h * SB_HEAD_DIM:(h + 1) * SB_HEAD_DIM]
            c_scr[h] = c_in_ref[0, h][:, 0:1]

    @pl.when(jnp.maximum(jnp.max(c_scr[0]), jnp.max(c_scr[1])) > SB_DEAD_LOG)
    def _():
        for h in range(2):
            qh = q_ref[0][:, h * SB_HEAD_DIM:(h + 1) * SB_HEAD_DIM]
            lb, ex, tot = _sb_scores(_dot(qh, kc_ref[0, h].astype(BF16)), nu_ref[...], None)
            c = c_scr[h]
            acc_scr[h] += _dot_nt(_sb_weights(lb, ex, c), vc_ref[0, h].astype(BF16))
            c_scr[h] = c + tot

    @pl.when(s == pl.num_programs(2) - 1)
    def _():
        acc_ref[0] = jnp.concatenate([acc_scr[0], acc_scr[1]], axis=1)


def _sb_sample(q, kn, vn, kct, vct):
    b, t, _ = q.shape
    past = kct.shape[3]
    tk = SB_BLOCK
    near = min(SB_SAMPLE_NEAR, past)
    hb = SB_SAMPLE_HEADS
    ng = SB_HEADS // hb
    new_spec = pl.BlockSpec((1, t, hb * SB_HEAD_DIM), lambda i, h: (i, 0, h))
    cache_spec = pl.BlockSpec((1, hb, SB_HEAD_DIM, near), lambda i, h: (i, h, 0, past // near - 1))
    nun, nup = _neg_upper(t), _neg_upper(tk)
    acc, c, flag = pl.pallas_call(
        functools.partial(_sb_sample_kernel, t=t, hb=hb, nblk=near // tk, tk=tk),
        grid=(b, ng),
        in_specs=[new_spec, new_spec, new_spec, cache_spec, cache_spec,
                  pl.BlockSpec((t, t), lambda i, h: (0, 0)),
                  pl.BlockSpec((tk, tk), lambda i, h: (0, 0))],
        out_specs=[new_spec,
                   pl.BlockSpec((1, hb, t, LANES), lambda i, h: (i, h, 0, 0)),
                   pl.BlockSpec((1, 1, SUBLANES, LANES), lambda i, h: (i, h, 0, 0))],
        out_shape=[jax.ShapeDtypeStruct((b, t, D_MODEL), F32),
                   jax.ShapeDtypeStruct((b, SB_HEADS, t, LANES), F32),
                   jax.ShapeDtypeStruct((b, ng, SUBLANES, LANES), F32)],
        compiler_params=_params(("arbitrary", "arbitrary")),
        name="sb_sample",
    )(q, kn, vn, kct, vct, nun, nup)
    nfar = (past - near) // tk
    if nfar == 0:
        return acc

    def tail():
        pair = pl.BlockSpec((1, t, 2 * SB_HEAD_DIM), lambda i, h, s: (i, 0, h))
        far = pl.BlockSpec((1, 2, SB_HEAD_DIM, tk), lambda i, h, s: (i, h, 0, nfar - 1 - s))
        return pl.pallas_call(
            _sb_tail_kernel,
            grid=(b, SB_HEADS // 2, nfar),
            in_specs=[pair, far, far, pl.BlockSpec((tk, tk), lambda i, h, s: (0, 0)), pair,
                      pl.BlockSpec((1, 2, t, LANES), lambda i, h, s: (i, h, 0, 0))],
            out_specs=pair,
            out_shape=jax.ShapeDtypeStruct((b, t, D_MODEL), F32),
            scratch_shapes=[pltpu.VMEM((2, t, SB_HEAD_DIM), F32), pltpu.VMEM((2, t, 1), F32)],
            compiler_params=_params(("arbitrary", "arbitrary", "arbitrary")),
            name="sb_tail",
        )(q, kct, vct, nup, acc, c)

    return lax.cond(jnp.max(flag) > SB_DEAD_LOG, tail, lambda: acc)


CONV_ROWS = 128
CONV_LANES = 128


def _mix_kernel(x_ref, osb_ref, left_ref, mk_ref, mv_ref, gpre_ref, wa_ref, wb_ref,
                wsb_ref, wcv_ref, wmo_ref, wout_ref, dww_ref, dwb_ref, lng_ref, lnb_ref, gpost_ref,
                y_ref, ulast_ref, hb_ref, uf_ref, cc_ref, qm_ref, om_ref, mg_ref, gb_ref, *, G, tt):
    s = pl.program_id(1)
    rows = G * tt
    H0 = CONV_HALO
    off = H0 - (CONV_WIDTH - 1)
    x = x_ref[...].reshape(rows, D_MODEL)
    hb_ref[...] = _rms(x, gpre_ref[...]).astype(BF16)

    def proj(w_ref, grp):
        return _dot(hb_ref[...], w_ref[:, grp * D_MODEL:(grp + 1) * D_MODEL])

    @pl.when(s == 0)
    def _():
        uf_ref[:, 0:H0, :] = left_ref[...]

    @pl.when(s > 0)
    def _():
        uf_ref[:, 0:H0, :] = uf_ref[:, tt:tt + H0, :]

    uf_ref[:, H0:H0 + tt, :] = (proj(wa_ref, 0) * _sigmoid(proj(wa_ref, 1))).reshape(G, tt, D_MODEL)
    ulast_ref[...] = uf_ref[:, tt:tt + H0, :]

    qm_ref[...] = (proj(wa_ref, 2) * (MEM_HEAD_DIM ** -0.5)).astype(BF16)
    for g in range(G):
        for h in range(MEM_HEADS):
            cs = slice(h * MEM_HEAD_DIM, (h + 1) * MEM_HEAD_DIM)
            sc = _dot_nt(qm_ref[g * tt:(g + 1) * tt, cs], mk_ref[g, h].astype(BF16))
            p = jnp.exp(sc - jnp.max(sc, axis=-1, keepdims=True))
            den = jnp.sum(p, axis=-1, keepdims=True)
            oh = _dot(p.astype(BF16), mv_ref[g, h].astype(BF16)) * (1.0 / den)
            om_ref[g * tt:(g + 1) * tt, cs] = oh.astype(BF16)
    y_sb = _dot(osb_ref[...].reshape(rows, D_MODEL).astype(BF16), wsb_ref[...])
    mg_ref[...] = _sigmoid(proj(wb_ref, 0)) * y_sb
    mg_ref[...] += _sigmoid(proj(wb_ref, 2)) * _dot(om_ref[...], wmo_ref[...])
    gb_ref[...] = _sigmoid(proj(wb_ref, 1))

    rc = min(CONV_ROWS, tt)
    win = rc + H0
    for g in range(G):
        for r0 in range(0, tt, rc):
            for c0 in range(0, D_MODEL, CONV_LANES):
                cs = slice(c0, c0 + CONV_LANES)
                window = uf_ref[g, r0:r0 + win, cs]
                acc = jnp.broadcast_to(dwb_ref[:, cs], (rc, CONV_LANES))
                for r in range(SUBLANES):
                    shifted = window if r == 0 else pltpu.roll(window, win - r, axis=0)
                    for m in range(H0 // SUBLANES + 1):
                        j = SUBLANES * m + r - off
                        if 0 <= j < CONV_WIDTH:
                            acc = acc + shifted[SUBLANES * m:SUBLANES * m + rc] * dww_ref[j:j + 1, cs]
                cc_ref[g * tt + r0:g * tt + r0 + rc, cs] = acc

    cc = cc_ref[...]
    mu = jnp.mean(cc, axis=-1, keepdims=True)
    d = cc - mu
    var = jnp.mean(d * d, axis=-1, keepdims=True)
    yn = d * lax.rsqrt(var + NORM_EPS) * lng_ref[...] + lnb_ref[...]
    y_conv = _dot((yn * _sigmoid(yn)).astype(BF16), wcv_ref[...])

    merged = mg_ref[...] + gb_ref[...] * y_conv
    out = _dot(merged.astype(BF16), wout_ref[...])
    y_ref[...] = (x + _rms(out, gpost_ref[...])).reshape(G, tt, D_MODEL)


def _mix(x, osb, left, mk, mv, gpre, w_in, wsb, wcv, wmo, wout, dww, dwb, lng, lnb, gpost, *, G, tt):
    b, t, _ = x.shape
    rows = G * tt
    tok = pl.BlockSpec((G, tt, D_MODEL), lambda i, s: (i, s, 0))
    halo = pl.BlockSpec((G, CONV_HALO, D_MODEL), lambda i, s: (i, 0, 0))
    mem = pl.BlockSpec((G, MEM_HEADS, MEM_TOKENS, MEM_HEAD_DIM), lambda i, s: (i, 0, 0, 0))
    wsq = _resident((D_MODEL, D_MODEL), lambda i, s: (0, 0))
    vec = pl.BlockSpec((1, D_MODEL), lambda i, s: (0, 0))
    return pl.pallas_call(
        functools.partial(_mix_kernel, G=G, tt=tt),
        grid=(b // G, t // tt),
        in_specs=[tok, tok, halo, mem, mem, vec,
                  _resident((D_MODEL, 3 * D_MODEL), lambda i, s: (0, 1)),
                  _resident((D_MODEL, 3 * D_MODEL), lambda i, s: (0, 2)),
                  wsq, wsq, wsq, wsq,
                  pl.BlockSpec((CONV_WIDTH, D_MODEL), lambda i, s: (0, 0)),
                  vec, vec, vec, vec],
        out_specs=[tok, halo],
        out_shape=[jax.ShapeDtypeStruct((b, t, D_MODEL), F32),
                   jax.ShapeDtypeStruct((b, CONV_HALO, D_MODEL), F32)],
        scratch_shapes=[pltpu.VMEM((rows, D_MODEL), BF16),
                        pltpu.VMEM((G, CONV_HALO + tt, D_MODEL), F32),
                        pltpu.VMEM((rows, D_MODEL), F32),
                        pltpu.VMEM((rows, D_MODEL), BF16),
                        pltpu.VMEM((rows, D_MODEL), BF16),
                        pltpu.VMEM((rows, D_MODEL), F32),
                        pltpu.VMEM((rows, D_MODEL), F32)],
        compiler_params=_params(("arbitrary", "arbitrary")),
        name="mix",
    )(x, osb, left, mk, mv, gpre, w_in, w_in, wsb, wcv, wmo, wout, dww, dwb, lng, lnb, gpost)


FFN_ROWS = 256
FFN_COLS = 512


def _gelu_tanh(x):
    return 0.5 * x * (1.0 + jnp.tanh(0.7978845608028654 * (x + 0.044715 * (x * x * x))))


def _ffn_kernel(x_ref, lg_ref, lv_ref, gpre_ref, gpost_ref, wg_ref, wv_ref, dwg_ref, dwv_ref, wd_ref,
                y_ref, st_ref, hb_ref, acc_ref, upf_ref, carry_ref, *, G, tt, nf):
    s = pl.program_id(1)
    f = pl.program_id(2)
    rows = G * tt
    S = SUBLANES
    lefts = (lg_ref, lv_ref)
    ws = (wg_ref, wv_ref)
    dws = (dwg_ref, dwv_ref)

    @pl.when(f == 0)
    def _():
        hb_ref[...] = _rms(x_ref[...].reshape(rows, D_MODEL), gpre_ref[...]).astype(BF16)
        acc_ref[...] = jnp.zeros_like(acc_ref)

    for half in range(2):
        @pl.when(s == 0)
        def _():
            upf_ref[half, :, 0:S, :] = lefts[half][...]

        @pl.when(s > 0)
        def _():
            upf_ref[half, :, 0:S, :] = carry_ref[f, half]

    rc = min(FFN_ROWS, rows)
    gpc = max(rc // tt, 1)
    tc = min(rc, tt)
    tf = wd_ref.shape[0]
    for r0 in range(0, rows, rc):
        g0, t0 = r0 // tt, r0 % tt
        hb = hb_ref[r0:r0 + rc, :]
        acts = []
        for c0 in range(0, tf, FFN_COLS):
            cs = slice(c0, min(c0 + FFN_COLS, tf))
            conv = []
            for half in range(2):
                up = _dot(hb, ws[half][:, cs]).reshape(gpc, tc, -1)
                upf_ref[half, g0:g0 + gpc, S + t0:S + t0 + tc, cs] = up
                dw = dws[half]
                c = (up * dw[2:3, cs]
                     + upf_ref[half, g0:g0 + gpc, S + t0 - 1:S + t0 - 1 + tc, cs] * dw[1:2, cs]
                     + upf_ref[half, g0:g0 + gpc, S + t0 - 2:S + t0 - 2 + tc, cs] * dw[0:1, cs])
                conv.append(c.reshape(rc, -1))
            acts.append((_gelu_tanh(conv[0]) * conv[1]).astype(BF16))
        acc_ref[r0:r0 + rc, :] += _dot(jnp.concatenate(acts, axis=1), wd_ref[...])

    for half in range(2):
        last = upf_ref[half, :, tt:tt + S, :]
        carry_ref[f, half] = last
        st_ref[half, f] = last

    @pl.when(f == nf - 1)
    def _():
        y = x_ref[...].reshape(rows, D_MODEL) + _rms(acc_ref[...], gpost_ref[...])
        y_ref[...] = y.reshape(G, tt, D_MODEL)


def _ffn(x, left, gpre, gpost, wup, dw, wdown, *, G, tt, nf):
    b, t, _ = x.shape
    rows = G * tt
    tf = FFN_DIM // nf
    S = SUBLANES
    tok = pl.BlockSpec((G, tt, D_MODEL), lambda i, s, f: (i, s, 0))
    vec = pl.BlockSpec((1, D_MODEL), lambda i, s, f: (0, 0))
    wspec = _resident if nf == 1 else pl.BlockSpec
    y, st = pl.pallas_call(
        functools.partial(_ffn_kernel, G=G, tt=tt, nf=nf),
        grid=(b // G, t // tt, nf),
        in_specs=[tok,
                  pl.BlockSpec((G, S, tf), lambda i, s, f: (i, 0, f)),
                  pl.BlockSpec((G, S, tf), lambda i, s, f: (i, 0, nf + f)),
                  vec, vec,
                  wspec((D_MODEL, tf), lambda i, s, f: (0, f)),
                  wspec((D_MODEL, tf), lambda i, s, f: (0, nf + f)),
                  pl.BlockSpec((FFN_CONV_WIDTH, tf), lambda i, s, f: (0, f)),
                  pl.BlockSpec((FFN_CONV_WIDTH, tf), lambda i, s, f: (0, nf + f)),
                  wspec((tf, D_MODEL), lambda i, s, f: (f, 0))],
        out_specs=[tok, pl.BlockSpec((2, nf, G, S, tf), lambda i, s, f: (0, 0, i, 0, 0))],
        out_shape=[jax.ShapeDtypeStruct((b, t, D_MODEL), F32),
                   jax.ShapeDtypeStruct((2, nf, b, S, tf), F32)],
        scratch_shapes=[pltpu.VMEM((rows, D_MODEL), BF16),
                        pltpu.VMEM((rows, D_MODEL), F32),
                        pltpu.VMEM((2, G, S + tt, tf), F32),
                        pltpu.VMEM((nf, 2, G, S, tf), F32)],
        compiler_params=_params(("arbitrary", "arbitrary", "arbitrary")),
        name="ffn",
    )(x, left, left, gpre, gpost, wup, wup, dw, dw, wdown)
    keep = FFN_CONV_WIDTH - 1
    state = st[:, :, :, S - keep:, :].transpose(2, 3, 0, 1, 4).reshape(b, keep, 2 * FFN_DIM)
    return y, state


def _tiles(b, t):
    if t >= 512:
        return {"qkv": (1, 1024), "mix": (1, 512), "ffn": (1, 512)}
    return {"qkv": (min(b, 512 // t), t), "mix": (min(b, 256 // t), t), "ffn": (min(b, 512 // t), t)}


def _layer(x, mk, mv, sb_cache, conv_left, ffn_left, w):
    b, t, _ = x.shape
    tiles = _tiles(b, t)
    prompt = sb_cache is None
    G, tt = tiles["qkv"]
    q, k, v, kf, vf = _qkv(x, w["g_mix_pre"], w["w_in"], G=G, tt=tt, dmajor=prompt)
    if prompt:
        osb = _sb_prompt(q, k, v, tq=SB_BLOCK)
        kf = kf.reshape(b, SB_HEADS, SB_HEAD_DIM, t).swapaxes(2, 3)
        vf = vf.reshape(b, SB_HEADS, SB_HEAD_DIM, t).swapaxes(2, 3)
    else:
        osb = _sb_sample(q, k, v, sb_cache[0].swapaxes(2, 3), sb_cache[1].swapaxes(2, 3))

    keep = CONV_WIDTH - 1
    left = jnp.pad(conv_left, ((0, 0), (CONV_HALO - keep, 0), (0, 0)))
    G, tt = tiles["mix"]
    x1, ulast = _mix(x, osb, left, mk, mv, w["g_mix_pre"], w["w_in"], w["w_sb_o"], w["w_conv_o"], w["w_mem_o"],
                     w["w_out"], w["conv_dw_w"], w["conv_dw_b"], w["conv_ln_g"], w["conv_ln_b"], w["g_mix_post"],
                     G=G, tt=tt)
    conv_state = ulast[:, CONV_HALO - keep:]

    fkeep = FFN_CONV_WIDTH - 1
    fleft = jnp.pad(ffn_left, ((0, 0), (SUBLANES - fkeep, 0), (0, 0)))
    G, tt = tiles["ffn"]
    y, ffn_state = _ffn(x1, fleft, w["g_ffn_pre"], w["g_ffn_post"], w["w_ffn_up"], w["ffn_dw_w"],
                        w["w_ffn_down"], G=G, tt=tt, nf=1)
    return y, kf, vf, conv_state, ffn_state


def kernel(x_prompt, x_sample, mem_prompt, cache_sb_k, cache_sb_v, state_conv, state_ffn_conv, cache_mem_k, cache_mem_v, g_mem, w_mem_kv, g_mix_pre, g_mix_post, w_in, w_sb_o, conv_dw_w, conv_dw_b, conv_ln_g, conv_ln_b, w_conv_o, w_mem_o, w_out, g_ffn_pre, g_ffn_post, w_ffn_up, ffn_dw_w, w_ffn_down):
    depth = w_in.shape[0]
    bp = x_prompt.shape[0]
    yp, ys = x_prompt, x_sample
    outs = [[] for _ in range(10)]
    for l in range(depth):
        vec = lambda a: a[l].reshape(1, -1)
        w = {
            "g_mix_pre": vec(g_mix_pre), "g_mix_post": vec(g_mix_post),
            "w_in": w_in[l].astype(BF16), "w_sb_o": w_sb_o[l].astype(BF16),
            "conv_dw_w": conv_dw_w[l], "conv_dw_b": vec(conv_dw_b),
            "conv_ln_g": vec(conv_ln_g), "conv_ln_b": vec(conv_ln_b),
            "w_conv_o": w_conv_o[l].astype(BF16), "w_mem_o": w_mem_o[l].astype(BF16),
            "w_out": w_out[l].astype(BF16),
            "g_ffn_pre": vec(g_ffn_pre), "g_ffn_post": vec(g_ffn_post),
            "w_ffn_up": w_ffn_up[l].astype(BF16), "ffn_dw_w": ffn_dw_w[l],
            "w_ffn_down": w_ffn_down[l].astype(BF16),
        }
        mkv, mkv_bf = _memkv(mem_prompt, vec(g_mem), w_mem_kv[l].astype(BF16))
        yp, kp, vp, cp, fp = _layer(
            yp, mkv_bf[0], mkv_bf[1], None,
            jnp.zeros((bp, CONV_WIDTH - 1, D_MODEL), F32),
            jnp.zeros((bp, FFN_CONV_WIDTH - 1, 2 * FFN_DIM), F32), w)
        ys, ks, vs, cs, fs = _layer(
            ys, cache_mem_k[l], cache_mem_v[l], (cache_sb_k[l], cache_sb_v[l]),
            state_conv[l], state_ffn_conv[l], w)
        for lst, val in zip(outs, (kp, vp, ks, vs, cp, cs, fp, fs, mkv[0], mkv[1])):
            lst.append(val)
    return (yp, ys) + tuple(jnp.stack(o) for o in outs)
```

```python
import functools

import jax
import jax.numpy as jnp
from jax import lax
from jax.experimental import pallas as pl
from jax.experimental.pallas import tpu as pltpu

F32 = jnp.float32
BF16 = jnp.bfloat16

D_MODEL = 1024
SB_HEADS = 16
SB_HEAD_DIM = 64
MEM_HEADS = 4
MEM_HEAD_DIM = 256
MEM_TOKENS = 256
CONV_WIDTH = 31
FFN_DIM = 2816
FFN_CONV_WIDTH = 3
NORM_EPS = 1e-6

SUBLANES = 8
LANES = 128
CONV_HALO = 32
VMEM_LIMIT_BYTES = 56 * 1024 * 1024

SB_DEAD_LOG = -110.0
SB_MASKED_SCORE = -1e4
SB_BLOCK = 256
SB_PROMPT_HEADS = 16
SB_STACK = 4
SB_SAMPLE_HEADS = 16
SB_SAMPLE_NEAR = 256


def _rms(x, g):
    return x * lax.rsqrt(jnp.mean(x * x, axis=-1, keepdims=True) + NORM_EPS) * g


def _sigmoid(x):
    return 0.5 * jnp.tanh(0.5 * x) + 0.5


def _softplus(z):
    sign = jnp.uint32(0x80000000)
    neg_abs = lax.bitcast_convert_type(lax.bitcast_convert_type(z, jnp.uint32) | sign, F32)
    return jnp.maximum(z, 0.0) + jnp.log(1.0 + jnp.exp(neg_abs))


def _dot(a, b):
    return jnp.dot(a, b, preferred_element_type=F32)


def _dot_nt(a, b):
    return lax.dot_general(a, b, (((1,), (1,)), ((), ())), preferred_element_type=F32)


def _params(sem):
    return pltpu.CompilerParams(dimension_semantics=sem, vmem_limit_bytes=VMEM_LIMIT_BYTES)


def _resident(block, index_map):
    return pl.BlockSpec(block, index_map, pipeline_mode=pl.Buffered(1))


def _memkv_kernel(mem_ref, g_ref, w_ref, o_ref, ob_ref):
    hb = _rms(mem_ref[0], g_ref[...]).astype(BF16)
    for kv in range(2):
        for h in range(MEM_HEADS):
            c0 = (kv * MEM_HEADS + h) * MEM_HEAD_DIM
            res = _dot(hb, w_ref[:, c0:c0 + MEM_HEAD_DIM])
            o_ref[kv, 0, h] = res
            ob_ref[kv, 0, h] = res.astype(BF16)


def _memkv(mem, g, w_bf):
    b = mem.shape[0]
    block = (2, 1, MEM_HEADS, MEM_TOKENS, MEM_HEAD_DIM)
    out_spec = pl.BlockSpec(block, lambda i: (0, i, 0, 0, 0))
    shape = (2, b, MEM_HEADS, MEM_TOKENS, MEM_HEAD_DIM)
    return pl.pallas_call(
        _memkv_kernel,
        grid=(b,),
        in_specs=[
            pl.BlockSpec((1, MEM_TOKENS, D_MODEL), lambda i: (i, 0, 0)),
            pl.BlockSpec((1, D_MODEL), lambda i: (0, 0)),
            _resident((D_MODEL, 2 * MEM_HEADS * MEM_HEAD_DIM), lambda i: (0, 0)),
        ],
        out_specs=[out_spec, out_spec],
        out_shape=[jax.ShapeDtypeStruct(shape, F32), jax.ShapeDtypeStruct(shape, BF16)],
        compiler_params=_params(("arbitrary",)),
        name="memkv",
    )(mem, g, w_bf)


QKV_COLS = 512


def _qkv_kernel(x_ref, g_ref, w_ref, q_ref, k_ref, v_ref, kf_ref, vf_ref, hb_ref, *, G, tt, dmajor):
    rows = G * tt
    cc = QKV_COLS
    hb_ref[...] = _rms(x_ref[...].reshape(rows, D_MODEL), g_ref[...]).astype(BF16)

    def tok(ref, n0, val):
        ref[:, :, n0:n0 + cc] = val.reshape(G, tt, cc).astype(ref.dtype)

    def heads(ref, n0, res):
        if dmajor:
            ref[0, n0:n0 + cc, :] = res.T
        else:
            for g in range(G):
                for hh in range(cc // SB_HEAD_DIM):
                    ref[g, n0 // SB_HEAD_DIM + hh] = res[g * tt:(g + 1) * tt, hh * SB_HEAD_DIM:(hh + 1) * SB_HEAD_DIM]

    for grp, (tref, href) in enumerate(((q_ref, None), (k_ref, kf_ref), (v_ref, vf_ref))):
        for n0 in range(0, D_MODEL, cc):
            res = _dot(hb_ref[...], w_ref[:, grp * D_MODEL + n0:grp * D_MODEL + n0 + cc])
            tok(tref, n0, res * (SB_HEAD_DIM ** -0.5) if href is None else res)
            if href is not None:
                heads(href, n0, res)


def _qkv(x, g, w_bf, *, G, tt, dmajor):
    b, t, _ = x.shape
    rows = G * tt
    tok_spec = pl.BlockSpec((G, tt, D_MODEL), lambda i, s: (i, s, 0))
    if dmajor:
        head_spec = pl.BlockSpec((G, D_MODEL, tt), lambda i, s: (i, 0, s))
        head_shape = jax.ShapeDtypeStruct((b, D_MODEL, t), F32)
    else:
        head_spec = pl.BlockSpec((G, SB_HEADS, tt, SB_HEAD_DIM), lambda i, s: (i, 0, s, 0))
        head_shape = jax.ShapeDtypeStruct((b, SB_HEADS, t, SB_HEAD_DIM), F32)
    tok_bf = jax.ShapeDtypeStruct((b, t, D_MODEL), BF16)
    return pl.pallas_call(
        functools.partial(_qkv_kernel, G=G, tt=tt, dmajor=dmajor),
        grid=(b // G, t // tt),
        in_specs=[tok_spec,
                  pl.BlockSpec((1, D_MODEL), lambda i, s: (0, 0)),
                  _resident((D_MODEL, 3 * D_MODEL), lambda i, s: (0, 0))],
        out_specs=[tok_spec, tok_spec, tok_spec, head_spec, head_spec],
        out_shape=[tok_bf, tok_bf, tok_bf, head_shape, head_shape],
        scratch_shapes=[pltpu.VMEM((rows, D_MODEL), BF16)],
        compiler_params=_params(("arbitrary", "arbitrary")),
        name="qkv",
    )(x, g, w_bf)


def _neg_upper(n):
    j = lax.broadcasted_iota(jnp.int32, (n, n), 0)
    s = lax.broadcasted_iota(jnp.int32, (n, n), 1)
    return jnp.where(j > s, -1.0, 0.0).astype(BF16)


def _strict_lower(n):
    r = lax.broadcasted_iota(jnp.int32, (n, n), 0)
    s = lax.broadcasted_iota(jnp.int32, (n, n), 1)
    return s < r


def _sb_mask(z, mask):
    return z if mask is None else jnp.where(mask, z, SB_MASKED_SCORE)


def _sb_scores(z, nu, mask):
    z = _sb_mask(z, mask)
    nl = _softplus(z)
    log_beta = z - nl
    excl = _dot(nl.astype(BF16), nu)
    tot = excl[:, 0:1] - nl[:, 0:1]
    return log_beta, excl, tot


def _sb_weights(log_beta, excl, c):
    return jnp.exp(log_beta + excl + c).astype(BF16)


def _sb_prompt_kernel(q_ref, k_ref, v_ref, nu_ref, o_ref, acc_ref, c_ref, *, tq):
    qi = pl.program_id(2)
    nu = nu_ref[...]
    mask = _strict_lower(tq)
    q = q_ref[0]
    hsl = [slice(h * SB_HEAD_DIM, (h + 1) * SB_HEAD_DIM) for h in range(2)]

    def kv(kb):
        start = pl.multiple_of(kb * tq, tq)
        return k_ref[0, pl.ds(start, tq), :], v_ref[0, pl.ds(start, tq), :]

    def cmax():
        return jnp.maximum(jnp.max(c_ref[0]), jnp.max(c_ref[1]))

    @pl.when(qi == 0)
    def _():
        kd, vd = kv(qi)
        for h in range(2):
            lb, ex, tot = _sb_scores(_dot_nt(q[:, hsl[h]], kd[:, hsl[h]]), nu, mask)
            acc_ref[h] = _dot(_sb_weights(lb, ex, 0.0), vd[:, hsl[h]])
            c_ref[h] = tot

    @pl.when(qi > 0)
    def _():
        kd, vd = kv(qi)
        kp, vp = kv(qi - 1)
        for h in range(2):
            lb, ex, tot = _sb_scores(_dot_nt(q[:, hsl[h]], kd[:, hsl[h]]), nu, mask)
            lbp, exp_, totp = _sb_scores(_dot_nt(q[:, hsl[h]], kp[:, hsl[h]]), nu, None)
            o = _dot(_sb_weights(lb, ex, 0.0), vd[:, hsl[h]])
            o = o + _dot(_sb_weights(lbp, exp_, tot), vp[:, hsl[h]])
            acc_ref[h] = o
            c_ref[h] = tot + totp

    def cond(carry):
        kb, cm = carry
        return jnp.logical_and(kb >= 0, cm > SB_DEAD_LOG)

    def body(carry):
        kb, _ = carry
        kk, vv = kv(kb)
        for h in range(2):
            lb, ex, tot = _sb_scores(_dot_nt(q[:, hsl[h]], kk[:, hsl[h]]), nu, None)
            c = c_ref[h]
            acc_ref[h] += _dot(_sb_weights(lb, ex, c), vv[:, hsl[h]])
            c_ref[h] = c + tot
        return kb - 1, cmax()

    lax.while_loop(cond, body, (qi - 2, cmax()))
    o_ref[0] = jnp.concatenate([acc_ref[0], acc_ref[1]], axis=1).astype(BF16)


def _sb_near_kernel(q_ref, kd_ref, kp_ref, vd_ref, vp_ref, nu_ref, o_ref, flag_ref, *, tq, hb):
    qi = pl.program_id(2)
    nu = nu_ref[...]
    mask = _strict_lower(tq)
    q, kd, vd = q_ref[0], kd_ref[0], vd_ref[0]
    hs = [slice(h * SB_HEAD_DIM, (h + 1) * SB_HEAD_DIM) for h in range(hb)]

    @pl.when(qi == 0)
    def _():
        outs = []
        for h in range(hb):
            lb, ex, _ = _sb_scores(_dot_nt(q[:, hs[h]], kd[:, hs[h]]), nu, mask)
            outs.append(_dot(_sb_weights(lb, ex, 0.0), vd[:, hs[h]]))
        o_ref[0] = jnp.concatenate(outs, axis=1).astype(BF16)
        flag_ref[...] = jnp.full(flag_ref.shape, 2.0 * SB_DEAD_LOG, F32)

    @pl.when(qi > 0)
    def _():
        kp, vp = kp_ref[0], vp_ref[0]
        outs = []
        cm = None
        for h0 in range(0, hb, SB_STACK):
            group = range(h0, h0 + SB_STACK)
            zs, nls = [], []
            for h in group:
                z = _dot_nt(q[:, hs[h]], jnp.concatenate([kd[:, hs[h]], kp[:, hs[h]]], axis=0))
                zs += [_sb_mask(z[:, 0:tq], mask), z[:, tq:2 * tq]]
            nls = [_softplus(z) for z in zs]
            excl = _dot(jnp.concatenate(nls, axis=0).astype(BF16), nu)
            for i, h in enumerate(group):
                zd, zp, nld, nlp = zs[2 * i], zs[2 * i + 1], nls[2 * i], nls[2 * i + 1]
                exd, exq = excl[2 * i * tq:(2 * i + 1) * tq], excl[(2 * i + 1) * tq:(2 * i + 2) * tq]
                tot = exd[:, 0:1] - nld[:, 0:1]
                totp = exq[:, 0:1] - nlp[:, 0:1]
                a = jnp.concatenate([_sb_weights(zd - nld, exd, 0.0), _sb_weights(zp - nlp, exq, tot)], axis=1)
                outs.append(_dot(a, jnp.concatenate([vd[:, hs[h]], vp[:, hs[h]]], axis=0)))
                m = jnp.max(tot + totp)
                cm = m if cm is None else jnp.maximum(cm, m)
        o_ref[0] = jnp.concatenate(outs, axis=1).astype(BF16)
        flag_ref[...] = jnp.full(flag_ref.shape, jnp.where(qi > 1, cm, 2.0 * SB_DEAD_LOG), F32)


def _sb_prompt(q, k, v, *, tq):
    b, t, _ = q.shape
    hb = SB_PROMPT_HEADS
    ng = SB_HEADS // hb
    nq = t // tq
    lanes = hb * SB_HEAD_DIM
    cur = pl.BlockSpec((1, tq, lanes), lambda i, h, s: (i, s, h))
    prev = pl.BlockSpec((1, tq, lanes), lambda i, h, s: (i, jnp.maximum(s - 1, 0), h))
    o, flag = pl.pallas_call(
        functools.partial(_sb_near_kernel, tq=tq, hb=hb),
        grid=(b, ng, nq),
        in_specs=[cur, cur, prev, cur, prev, pl.BlockSpec((tq, tq), lambda i, h, s: (0, 0))],
        out_specs=[cur, pl.BlockSpec((1, 1, 1, SUBLANES, LANES), lambda i, h, s: (i, h, s, 0, 0))],
        out_shape=[jax.ShapeDtypeStruct((b, t, D_MODEL), BF16),
                   jax.ShapeDtypeStruct((b, ng, nq, SUBLANES, LANES), F32)],
        compiler_params=_params(("arbitrary", "arbitrary", "arbitrary")),
        name="sb_near",
    )(q, k, k, v, v, _neg_upper(tq))
    return lax.cond(jnp.max(flag) > SB_DEAD_LOG, lambda: _sb_prompt_full(q, k, v, tq=tq), lambda: o)


def _sb_prompt_full(q, k, v, *, tq):
    b, t, _ = q.shape
    hp = SB_HEADS // 2
    lanes = 2 * SB_HEAD_DIM
    nu = _neg_upper(tq)
    return pl.pallas_call(
        functools.partial(_sb_prompt_kernel, tq=tq),
        grid=(b, hp, t // tq),
        in_specs=[
            pl.BlockSpec((1, tq, lanes), lambda i, h, s: (i, s, h)),
            pl.BlockSpec((1, t, lanes), lambda i, h, s: (i, 0, h)),
            pl.BlockSpec((1, t, lanes), lambda i, h, s: (i, 0, h)),
            pl.BlockSpec((tq, tq), lambda i, h, s: (0, 0)),
        ],
        out_specs=pl.BlockSpec((1, tq, lanes), lambda i, h, s: (i, s, h)),
        out_shape=jax.ShapeDtypeStruct((b, t, D_MODEL), BF16),
        scratch_shapes=[pltpu.VMEM((2, tq, SB_HEAD_DIM), F32), pltpu.VMEM((2, tq, 1), F32)],
        compiler_params=_params(("arbitrary", "arbitrary", "arbitrary")),
        name="sb_prompt",
    )(q, k, v, nu)


def _sb_sample_kernel(q_ref, kn_ref, vn_ref, kc_ref, vc_ref, nun_ref, nup_ref, acc_ref, c_ref, flag_ref,
                      *, t, hb, nblk, tk):
    mask = _strict_lower(t)
    q, kn, vn = q_ref[0], kn_ref[0], vn_ref[0]
    hs = [slice(h * SB_HEAD_DIM, (h + 1) * SB_HEAD_DIM) for h in range(hb)]
    ks = [slice(kb * tk, (kb + 1) * tk) for kb in range(nblk - 1, -1, -1)]

    lbs, nls = [], []
    for h in range(hb):
        qh = q[:, hs[h]]
        zs = [_sb_mask(_dot_nt(qh, kn[:, hs[h]]), mask)]
        zs += [_dot(qh, kc_ref[0, h, :, s].astype(BF16)) for s in ks]
        nl = [_softplus(z) for z in zs]
        lbs.append([z - n for z, n in zip(zs, nl)])
        nls.append(nl)

    excl = []
    for sg in range(nblk + 1):
        stack = jnp.concatenate([nls[h][sg] for h in range(hb)], axis=0).astype(BF16)
        excl.append(_dot(stack, nun_ref[...] if sg == 0 else nup_ref[...]))

    cm = None
    for h in range(hb):
        rs = slice(h * t, (h + 1) * t)
        ex = excl[0][rs]
        o = _dot(_sb_weights(lbs[h][0], ex, 0.0), vn[:, hs[h]])
        c = ex[:, 0:1] - nls[h][0][:, 0:1]
        for i, s in enumerate(ks):
            ex = excl[i + 1][rs]
            o = o + _dot_nt(_sb_weights(lbs[h][i + 1], ex, c), vc_ref[0, h, :, s].astype(BF16))
            c = c + (ex[:, 0:1] - nls[h][i + 1][:, 0:1])
        acc_ref[0, :, hs[h]] = o
        c_ref[0, h] = jnp.broadcast_to(c, (t, LANES))
        m = jnp.max(c)
        cm = m if cm is None else jnp.maximum(cm, m)
    flag_ref[...] = jnp.full(flag_ref.shape, cm, F32)


def _sb_tail_kernel(q_ref, kc_ref, vc_ref, nu_ref, acc_in_ref, c_in_ref, acc_ref, acc_scr, c_scr):
    s = pl.program_id(2)

    @pl.when(s == 0)
    def _():
        for h in range(2):
            acc_scr[h] = acc_in_ref[0][:, h * SB_HEAD_DIM:(h + 1) * SB_HEAD_DIM]
            c_scr[h] = c_in_ref[0, h][:, 0:1]

    @pl.when(jnp.maximum(jnp.max(c_scr[0]), jnp.max(c_scr[1])) > SB_DEAD_LOG)
    def _():
        for h in range(2):
            qh = q_ref[0][:, h * SB_HEAD_DIM:(h + 1) * SB_HEAD_DIM]
            lb, ex, tot = _sb_scores(_dot(qh, kc_ref[0, h].astype(BF16)), nu_ref[...], None)
            c = c_scr[h]
            acc_scr[h] += _dot_nt(_sb_weights(lb, ex, c), vc_ref[0, h].astype(BF16))
            c_scr[h] = c + tot

    @pl.when(s == pl.num_programs(2) - 1)
    def _():
        acc_ref[0] = jnp.concatenate([acc_scr[0], acc_scr[1]], axis=1)


def _sb_sample(q, kn, vn, kct, vct):
    b, t, _ = q.shape
    past = kct.shape[3]
    tk = SB_BLOCK
    near = min(SB_SAMPLE_NEAR, past)
    hb = SB_SAMPLE_HEADS
    ng = SB_HEADS // hb
    new_spec = pl.BlockSpec((1, t, hb * SB_HEAD_DIM), lambda i, h: (i, 0, h))
    cache_spec = pl.BlockSpec((1, hb, SB_HEAD_DIM, near), lambda i, h: (i, h, 0, past // near - 1))
    nun, nup = _neg_upper(t), _neg_upper(tk)
    acc, c, flag = pl.pallas_call(
        functools.partial(_sb_sample_kernel, t=t, hb=hb, nblk=near // tk, tk=tk),
        grid=(b, ng),
        in_specs=[new_spec, new_spec, new_spec, cache_spec, cache_spec,
                  pl.BlockSpec((t, t), lambda i, h: (0, 0)),
                  pl.BlockSpec((tk, tk), lambda i, h: (0, 0))],
        out_specs=[new_spec,
                   pl.BlockSpec((1, hb, t, LANES), lambda i, h: (i, h, 0, 0)),
                   pl.BlockSpec((1, 1, SUBLANES, LANES), lambda i, h: (i, h, 0, 0))],
        out_shape=[jax.ShapeDtypeStruct((b, t, D_MODEL), F32),
                   jax.ShapeDtypeStruct((b, SB_HEADS, t, LANES), F32),
                   jax.ShapeDtypeStruct((b, ng, SUBLANES, LANES), F32)],
        compiler_params=_params(("arbitrary", "arbitrary")),
        name="sb_sample",
    )(q, kn, vn, kct, vct, nun, nup)
    nfar = (past - near) // tk
    if nfar == 0:
        return acc

    def tail():
        pair = pl.BlockSpec((1, t, 2 * SB_HEAD_DIM), lambda i, h, s: (i, 0, h))
        far = pl.BlockSpec((1, 2, SB_HEAD_DIM, tk), lambda i, h, s: (i, h, 0, nfar - 1 - s))
        return pl.pallas_call(
            _sb_tail_kernel,
            grid=(b, SB_HEADS // 2, nfar),
            in_specs=[pair, far, far, pl.BlockSpec((tk, tk), lambda i, h, s: (0, 0)), pair,
                      pl.BlockSpec((1, 2, t, LANES), lambda i, h, s: (i, h, 0, 0))],
            out_specs=pair,
            out_shape=jax.ShapeDtypeStruct((b, t, D_MODEL), F32),
            scratch_shapes=[pltpu.VMEM((2, t, SB_HEAD_DIM), F32), pltpu.VMEM((2, t, 1), F32)],
            compiler_params=_params(("arbitrary", "arbitrary", "arbitrary")),
            name="sb_tail",
        )(q, kct, vct, nup, acc, c)

    return lax.cond(jnp.max(flag) > SB_DEAD_LOG, tail, lambda: acc)


CONV_ROWS = 128
CONV_LANES = 128


def _mix_kernel(x_ref, osb_ref, left_ref, mk_ref, mv_ref, gpre_ref, wa_ref, wb_ref,
                wsb_ref, wcv_ref, wmo_ref, wout_ref, dww_ref, dwb_ref, lng_ref, lnb_ref, gpost_ref,
                y_ref, ulast_ref, hb_ref, uf_ref, cc_ref, qm_ref, om_ref, mg_ref, gb_ref, *, G, tt):
    s = pl.program_id(1)
    rows = G * tt
    H0 = CONV_HALO
    off = H0 - (CONV_WIDTH - 1)
    x = x_ref[...].reshape(rows, D_MODEL)
    hb_ref[...] = _rms(x, gpre_ref[...]).astype(BF16)

    def proj(w_ref, grp):
        return _dot(hb_ref[...], w_ref[:, grp * D_MODEL:(grp + 1) * D_MODEL])

    @pl.when(s == 0)
    def _():
        uf_ref[:, 0:H0, :] = left_ref[...]

    @pl.when(s > 0)
    def _():
        uf_ref[:, 0:H0, :] = uf_ref[:, tt:tt + H0, :]

    uf_ref[:, H0:H0 + tt, :] = (proj(wa_ref, 0) * _sigmoid(proj(wa_ref, 1))).reshape(G, tt, D_MODEL)
    ulast_ref[...] = uf_ref[:, tt:tt + H0, :]

    qm_ref[...] = (proj(wa_ref, 2) * (MEM_HEAD_DIM ** -0.5)).astype(BF16)
    for g in range(G):
        for h in range(MEM_HEADS):
            cs = slice(h * MEM_HEAD_DIM, (h + 1) * MEM_HEAD_DIM)
            sc = _dot_nt(qm_ref[g * tt:(g + 1) * tt, cs], mk_ref[g, h].astype(BF16))
            p = jnp.exp(sc - jnp.max(sc, axis=-1, keepdims=True))
            den = jnp.sum(p, axis=-1, keepdims=True)
            oh = _dot(p.astype(BF16), mv_ref[g, h].astype(BF16)) * (1.0 / den)
            om_ref[g * tt:(g + 1) * tt, cs] = oh.astype(BF16)
    y_sb = _dot(osb_ref[...].reshape(rows, D_MODEL).astype(BF16), wsb_ref[...])
    mg_ref[...] = _sigmoid(proj(wb_ref, 0)) * y_sb
    mg_ref[...] += _sigmoid(proj(wb_ref, 2)) * _dot(om_ref[...], wmo_ref[...])
    gb_ref[...] = _sigmoid(proj(wb_ref, 1))

    rc = min(CONV_ROWS, tt)
    win = rc + H0
    for g in range(G):
        for r0 in range(0, tt, rc):
            for c0 in range(0, D_MODEL, CONV_LANES):
                cs = slice(c0, c0 + CONV_LANES)
                window = uf_ref[g, r0:r0 + win, cs]
                acc = jnp.broadcast_to(dwb_ref[:, cs], (rc, CONV_LANES))
                for r in range(SUBLANES):
                    shifted = window if r == 0 else pltpu.roll(window, win - r, axis=0)
                    for m in range(H0 // SUBLANES + 1):
                        j = SUBLANES * m + r - off
                        if 0 <= j < CONV_WIDTH:
                            acc = acc + shifted[SUBLANES * m:SUBLANES * m + rc] * dww_ref[j:j + 1, cs]
                cc_ref[g * tt + r0:g * tt + r0 + rc, cs] = acc

    cc = cc_ref[...]
    mu = jnp.mean(cc, axis=-1, keepdims=True)
    d = cc - mu
    var = jnp.mean(d * d, axis=-1, keepdims=True)
    yn = d * lax.rsqrt(var + NORM_EPS) * lng_ref[...] + lnb_ref[...]
    y_conv = _dot((yn * _sigmoid(yn)).astype(BF16), wcv_ref[...])

    merged = mg_ref[...] + gb_ref[...] * y_conv
    out = _dot(merged.astype(BF16), wout_ref[...])
    y_ref[...] = (x + _rms(out, gpost_ref[...])).reshape(G, tt, D_MODEL)


def _mix(x, osb, left, mk, mv, gpre, w_in, wsb, wcv, wmo, wout, dww, dwb, lng, lnb, gpost, *, G, tt):
    b, t, _ = x.shape
    rows = G * tt
    tok = pl.BlockSpec((G, tt, D_MODEL), lambda i, s: (i, s, 0))
    halo = pl.BlockSpec((G, CONV_HALO, D_MODEL), lambda i, s: (i, 0, 0))
    mem = pl.BlockSpec((G, MEM_HEADS, MEM_TOKENS, MEM_HEAD_DIM), lambda i, s: (i, 0, 0, 0))
    wsq = _resident((D_MODEL, D_MODEL), lambda i, s: (0, 0))
    vec = pl.BlockSpec((1, D_MODEL), lambda i, s: (0, 0))
    return pl.pallas_call(
        functools.partial(_mix_kernel, G=G, tt=tt),
        grid=(b // G, t // tt),
        in_specs=[tok, tok, halo, mem, mem, vec,
                  _resident((D_MODEL, 3 * D_MODEL), lambda i, s: (0, 1)),
                  _resident((D_MODEL, 3 * D_MODEL), lambda i, s: (0, 2)),
                  wsq, wsq, wsq, wsq,
                  pl.BlockSpec((CONV_WIDTH, D_MODEL), lambda i, s: (0, 0)),
                  vec, vec, vec, vec],
        out_specs=[tok, halo],
        out_shape=[jax.ShapeDtypeStruct((b, t, D_MODEL), F32),
                   jax.ShapeDtypeStruct((b, CONV_HALO, D_MODEL), F32)],
        scratch_shapes=[pltpu.VMEM((rows, D_MODEL), BF16),
                        pltpu.VMEM((G, CONV_HALO + tt, D_MODEL), F32),
                        pltpu.VMEM((rows, D_MODEL), F32),
                        pltpu.VMEM((rows, D_MODEL), BF16),
                        pltpu.VMEM((rows, D_MODEL), BF16),
                        pltpu.VMEM((rows, D_MODEL), F32),
                        pltpu.VMEM((rows, D_MODEL), F32)],
        compiler_params=_params(("arbitrary", "arbitrary")),
        name="mix",
    )(x, osb, left, mk, mv, gpre, w_in, w_in, wsb, wcv, wmo, wout, dww, dwb, lng, lnb, gpost)


FFN_ROWS = 256
FFN_COLS = 512


def _gelu_tanh(x):
    return 0.5 * x * (1.0 + jnp.tanh(0.7978845608028654 * (x + 0.044715 * (x * x * x))))


def _ffn_kernel(x_ref, lg_ref, lv_ref, gpre_ref, gpost_ref, wg_ref, wv_ref, dwg_ref, dwv_ref, wd_ref,
                y_ref, st_ref, hb_ref, acc_ref, upf_ref, carry_ref, *, G, tt, nf):
    s = pl.program_id(1)
    f = pl.program_id(2)
    rows = G * tt
    S = SUBLANES
    lefts = (lg_ref, lv_ref)
    ws = (wg_ref, wv_ref)
    dws = (dwg_ref, dwv_ref)

    @pl.when(f == 0)
    def _():
        hb_ref[...] = _rms(x_ref[...].reshape(rows, D_MODEL), gpre_ref[...]).astype(BF16)
        acc_ref[...] = jnp.zeros_like(acc_ref)

    for half in range(2):
        @pl.when(s == 0)
        def _():
            upf_ref[half, :, 0:S, :] = lefts[half][...]

        @pl.when(s > 0)
        def _():
            upf_ref[half, :, 0:S, :] = carry_ref[f, half]

    rc = min(FFN_ROWS, rows)
    gpc = max(rc // tt, 1)
    tc = min(rc, tt)
    tf = wd_ref.shape[0]
    for r0 in range(0, rows, rc):
        g0, t0 = r0 // tt, r0 % tt
        hb = hb_ref[r0:r0 + rc, :]
        acts = []
        for c0 in range(0, tf, FFN_COLS):
            cs = slice(c0, min(c0 + FFN_COLS, tf))
            conv = []
            for half in range(2):
                up = _dot(hb, ws[half][:, cs]).reshape(gpc, tc, -1)
                upf_ref[half, g0:g0 + gpc, S + t0:S + t0 + tc, cs] = up
                dw = dws[half]
                c = (up * dw[2:3, cs]
                     + upf_ref[half, g0:g0 + gpc, S + t0 - 1:S + t0 - 1 + tc, cs] * dw[1:2, cs]
                     + upf_ref[half, g0:g0 + gpc, S + t0 - 2:S + t0 - 2 + tc, cs] * dw[0:1, cs])
                conv.append(c.reshape(rc, -1))
            acts.append((_gelu_tanh(conv[0]) * conv[1]).astype(BF16))
        acc_ref[r0:r0 + rc, :] += _dot(jnp.concatenate(acts, axis=1), wd_ref[...])

    for half in range(2):
        last = upf_ref[half, :, tt:tt + S, :]
        carry_ref[f, half] = last
        st_ref[half, f] = last

    @pl.when(f == nf - 1)
    def _():
        y = x_ref[...].reshape(rows, D_MODEL) + _rms(acc_ref[...], gpost_ref[...])
        y_ref[...] = y.reshape(G, tt, D_MODEL)


def _ffn(x, left, gpre, gpost, wup, dw, wdown, *, G, tt, nf):
    b, t, _ = x.shape
    rows = G * tt
    tf = FFN_DIM // nf
    S = SUBLANES
    tok = pl.BlockSpec((G, tt, D_MODEL), lambda i, s, f: (i, s, 0))
    vec = pl.BlockSpec((1, D_MODEL), lambda i, s, f: (0, 0))
    wspec = _resident if nf == 1 else pl.BlockSpec
    y, st = pl.pallas_call(
        functools.partial(_ffn_kernel, G=G, tt=tt, nf=nf),
        grid=(b // G, t // tt, nf),
        in_specs=[tok,
                  pl.BlockSpec((G, S, tf), lambda i, s, f: (i, 0, f)),
                  pl.BlockSpec((G, S, tf), lambda i, s, f: (i, 0, nf + f)),
                  vec, vec,
                  wspec((D_MODEL, tf), lambda i, s, f: (0, f)),
                  wspec((D_MODEL, tf), lambda i, s, f: (0, nf + f)),
                  pl.BlockSpec((FFN_CONV_WIDTH, tf), lambda i, s, f: (0, f)),
                  pl.BlockSpec((FFN_CONV_WIDTH, tf), lambda i, s, f: (0, nf + f)),
                  wspec((tf, D_MODEL), lambda i, s, f: (f, 0))],
        out_specs=[tok, pl.BlockSpec((2, nf, G, S, tf), lambda i, s, f: (0, 0, i, 0, 0))],
        out_shape=[jax.ShapeDtypeStruct((b, t, D_MODEL), F32),
                   jax.ShapeDtypeStruct((2, nf, b, S, tf), F32)],
        scratch_shapes=[pltpu.VMEM((rows, D_MODEL), BF16),
                        pltpu.VMEM((rows, D_MODEL), F32),
                        pltpu.VMEM((2, G, S + tt, tf), F32),
                        pltpu.VMEM((nf, 2, G, S, tf), F32)],
        compiler_params=_params(("arbitrary", "arbitrary", "arbitrary")),
        name="ffn",
    )(x, left, left, gpre, gpost, wup, wup, dw, dw, wdown)
    keep = FFN_CONV_WIDTH - 1
    state = st[:, :, :, S - keep:, :].transpose(2, 3, 0, 1, 4).reshape(b, keep, 2 * FFN_DIM)
    return y, state


def _tiles(b, t):
    if t >= 512:
        return {"qkv": (1, 1024), "mix": (1, 512), "ffn": (1, 512)}
    return {"qkv": (min(b, 512 // t), t), "mix": (min(b, 256 // t), t), "ffn": (min(b, 512 // t), t)}


def _layer(x, mk, mv, sb_cache, conv_left, ffn_left, w):
    b, t, _ = x.shape
    tiles = _tiles(b, t)
    prompt = sb_cache is None
    G, tt = tiles["qkv"]
    q, k, v, kf, vf = _qkv(x, w["g_mix_pre"], w["w_in"], G=G, tt=tt, dmajor=prompt)
    if prompt:
        osb = _sb_prompt(q, k, v, tq=SB_BLOCK)
        kf = kf.reshape(b, SB_HEADS, SB_HEAD_DIM, t).swapaxes(2, 3)
        vf = vf.reshape(b, SB_HEADS, SB_HEAD_DIM, t).swapaxes(2, 3)
    else:
        osb = _sb_sample(q, k, v, sb_cache[0].swapaxes(2, 3), sb_cache[1].swapaxes(2, 3))

    keep = CONV_WIDTH - 1
    left = jnp.pad(conv_left, ((0, 0), (CONV_HALO - keep, 0), (0, 0)))
    G, tt = tiles["mix"]
    x1, ulast = _mix(x, osb, left, mk, mv, w["g_mix_pre"], w["w_in"], w["w_sb_o"], w["w_conv_o"], w["w_mem_o"],
                     w["w_out"], w["conv_dw_w"], w["conv_dw_b"], w["conv_ln_g"], w["conv_ln_b"], w["g_mix_post"],
                     G=G, tt=tt)
    conv_state = ulast[:, CONV_HALO - keep:]

    fkeep = FFN_CONV_WIDTH - 1
    fleft = jnp.pad(ffn_left, ((0, 0), (SUBLANES - fkeep, 0), (0, 0)))
    G, tt = tiles["ffn"]
    y, ffn_state = _ffn(x1, fleft, w["g_ffn_pre"], w["g_ffn_post"], w["w_ffn_up"], w["ffn_dw_w"],
                        w["w_ffn_down"], G=G, tt=tt, nf=1)
    return y, kf, vf, conv_state, ffn_state


def kernel(x_prompt, x_sample, mem_prompt, cache_sb_k, cache_sb_v, state_conv, state_ffn_conv, cache_mem_k, cache_mem_v, g_mem, w_mem_kv, g_mix_pre, g_mix_post, w_in, w_sb_o, conv_dw_w, conv_dw_b, conv_ln_g, conv_ln_b, w_conv_o, w_mem_o, w_out, g_ffn_pre, g_ffn_post, w_ffn_up, ffn_dw_w, w_ffn_down):
    depth = w_in.shape[0]
    bp = x_prompt.shape[0]
    yp, ys = x_prompt, x_sample
    outs = [[] for _ in range(10)]
    for l in range(depth):
        vec = lambda a: a[l].reshape(1, -1)
        w = {
            "g_mix_pre": vec(g_mix_pre), "g_mix_post": vec(g_mix_post),
            "w_in": w_in[l].astype(BF16), "w_sb_o": w_sb_o[l].astype(BF16),
            "conv_dw_w": conv_dw_w[l], "conv_dw_b": vec(conv_dw_b),
            "conv_ln_g": vec(conv_ln_g), "conv_ln_b": vec(conv_ln_b),
            "w_conv_o": w_conv_o[l].astype(BF16), "w_mem_o": w_mem_o[l].astype(BF16),
            "w_out": w_out[l].astype(BF16),
            "g_ffn_pre": vec(g_ffn_pre), "g_ffn_post": vec(g_ffn_post),
            "w_ffn_up": w_ffn_up[l].astype(BF16), "ffn_dw_w": ffn_dw_w[l],
            "w_ffn_down": w_ffn_down[l].astype(BF16),
        }
        mkv, mkv_bf = _memkv(mem_prompt, vec(g_mem), w_mem_kv[l].astype(BF16))
        yp, kp, vp, cp, fp = _layer(
            yp, mkv_bf[0], mkv_bf[1], None,
            jnp.zeros((bp, CONV_WIDTH - 1, D_MODEL), F32),
            jnp.zeros((bp, FFN_CONV_WIDTH - 1, 2 * FFN_DIM), F32), w)
        ys, ks, vs, cs, fs = _layer(
            ys, cache_mem_k[l], cache_mem_v[l], (cache_sb_k[l], cache_sb_v[l]),
            state_conv[l], state_ffn_conv[l], w)
        for lst, val in zip(outs, (kp, vp, ks, vs, cp, cs, fp, fs, mkv[0], mkv[1])):
            lst.append(val)
    return (yp, ys) + tuple(jnp.stack(o) for o in outs)
```

```python
import functools

import jax
import jax.numpy as jnp
from jax import lax
from jax.experimental import pallas as pl
from jax.experimental.pallas import tpu as pltpu

F32 = jnp.float32
BF16 = jnp.bfloat16

D_MODEL = 1024
SB_HEADS = 16
SB_HEAD_DIM = 64
MEM_HEADS = 4
MEM_HEAD_DIM = 256
MEM_TOKENS = 256
CONV_WIDTH = 31
FFN_DIM = 2816
FFN_CONV_WIDTH = 3
NORM_EPS = 1e-6

SUBLANES = 8
LANES = 128
CONV_HALO = 32
VMEM_LIMIT_BYTES = 56 * 1024 * 1024

SB_DEAD_LOG = -110.0
SB_MASKED_SCORE = -1e4
SB_BLOCK = 256
SB_PROMPT_HEADS = 16
SB_STACK = 4
SB_SAMPLE_HEADS = 16
SB_SAMPLE_NEAR = 256


def _rms(x, g):
    return x * lax.rsqrt(jnp.mean(x * x, axis=-1, keepdims=True) + NORM_EPS) * g


def _sigmoid(x):
    return 0.5 * jnp.tanh(0.5 * x) + 0.5


def _softplus(z):
    sign = jnp.uint32(0x80000000)
    neg_abs = lax.bitcast_convert_type(lax.bitcast_convert_type(z, jnp.uint32) | sign, F32)
    return jnp.maximum(z, 0.0) + jnp.log(1.0 + jnp.exp(neg_abs))


def _dot(a, b):
    return jnp.dot(a, b, preferred_element_type=F32)


def _dot_nt(a, b):
    return lax.dot_general(a, b, (((1,), (1,)), ((), ())), preferred_element_type=F32)


def _params(sem):
    return pltpu.CompilerParams(dimension_semantics=sem, vmem_limit_bytes=VMEM_LIMIT_BYTES)


def _resident(block, index_map):
    return pl.BlockSpec(block, index_map, pipeline_mode=pl.Buffered(1))


def _memkv_kernel(mem_ref, g_ref, w_ref, o_ref, ob_ref):
    hb = _rms(mem_ref[0], g_ref[...]).astype(BF16)
    for kv in range(2):
        for h in range(MEM_HEADS):
            c0 = (kv * MEM_HEADS + h) * MEM_HEAD_DIM
            res = _dot(hb, w_ref[:, c0:c0 + MEM_HEAD_DIM])
            o_ref[kv, 0, h] = res
            ob_ref[kv, 0, h] = res.astype(BF16)


def _memkv(mem, g, w_bf):
    b = mem.shape[0]
    block = (2, 1, MEM_HEADS, MEM_TOKENS, MEM_HEAD_DIM)
    out_spec = pl.BlockSpec(block, lambda i: (0, i, 0, 0, 0))
    shape = (2, b, MEM_HEADS, MEM_TOKENS, MEM_HEAD_DIM)
    return pl.pallas_call(
        _memkv_kernel,
        grid=(b,),
        in_specs=[
            pl.BlockSpec((1, MEM_TOKENS, D_MODEL), lambda i: (i, 0, 0)),
            pl.BlockSpec((1, D_MODEL), lambda i: (0, 0)),
            _resident((D_MODEL, 2 * MEM_HEADS * MEM_HEAD_DIM), lambda i: (0, 0)),
        ],
        out_specs=[out_spec, out_spec],
        out_shape=[jax.ShapeDtypeStruct(shape, F32), jax.ShapeDtypeStruct(shape, BF16)],
        compiler_params=_params(("arbitrary",)),
        name="memkv",
    )(mem, g, w_bf)


QKV_COLS = 512


def _qkv_kernel(x_ref, g_ref, w_ref, q_ref, k_ref, v_ref, kf_ref, vf_ref, hb_ref, *, G, tt, dmajor):
    rows = G * tt
    cc = QKV_COLS
    hb_ref[...] = _rms(x_ref[...].reshape(rows, D_MODEL), g_ref[...]).astype(BF16)

    def tok(ref, n0, val):
        ref[:, :, n0:n0 + cc] = val.reshape(G, tt, cc).astype(ref.dtype)

    def heads(ref, n0, res):
        if dmajor:
            ref[0, n0:n0 + cc, :] = res.T
        else:
            for g in range(G):
                for hh in range(cc // SB_HEAD_DIM):
                    ref[g, n0 // SB_HEAD_DIM + hh] = res[g * tt:(g + 1) * tt, hh * SB_HEAD_DIM:(hh + 1) * SB_HEAD_DIM]

    for grp, (tref, href) in enumerate(((q_ref, None), (k_ref, kf_ref), (v_ref, vf_ref))):
        for n0 in range(0, D_MODEL, cc):
            res = _dot(hb_ref[...], w_ref[:, grp * D_MODEL + n0:grp * D_MODEL + n0 + cc])
            tok(tref, n0, res * (SB_HEAD_DIM ** -0.5) if href is None else res)
            if href is not None:
                heads(href, n0, res)


def _qkv(x, g, w_bf, *, G, tt, dmajor):
    b, t, _ = x.shape
    rows = G * tt
    tok_spec = pl.BlockSpec((G, tt, D_MODEL), lambda i, s: (i, s, 0))
    if dmajor:
        head_spec = pl.BlockSpec((G, D_MODEL, tt), lambda i, s: (i, 0, s))
        head_shape = jax.ShapeDtypeStruct((b, D_MODEL, t), F32)
    else:
        head_spec = pl.BlockSpec((G, SB_HEADS, tt, SB_HEAD_DIM), lambda i, s: (i, 0, s, 0))
        head_shape = jax.ShapeDtypeStruct((b, SB_HEADS, t, SB_HEAD_DIM), F32)
    tok_bf = jax.ShapeDtypeStruct((b, t, D_MODEL), BF16)
    return pl.pallas_call(
        functools.partial(_qkv_kernel, G=G, tt=tt, dmajor=dmajor),
        grid=(b // G, t // tt),
        in_specs=[tok_spec,
                  pl.BlockSpec((1, D_MODEL), lambda i, s: (0, 0)),
                  _resident((D_MODEL, 3 * D_MODEL), lambda i, s: (0, 0))],
        out_specs=[tok_spec, tok_spec, tok_spec, head_spec, head_spec],
        out_shape=[tok_bf, tok_bf, tok_bf, head_shape, head_shape],
        scratch_shapes=[pltpu.VMEM((rows, D_MODEL), BF16)],
        compiler_params=_params(("arbitrary", "arbitrary")),
        name="qkv",
    )(x, g, w_bf)


def _neg_upper(n):
    j = lax.broadcasted_iota(jnp.int32, (n, n), 0)
    s = lax.broadcasted_iota(jnp.int32, (n, n), 1)
    return jnp.where(j > s, -1.0, 0.0).astype(BF16)


def _strict_lower(n):
    r = lax.broadcasted_iota(jnp.int32, (n, n), 0)
    s = lax.broadcasted_iota(jnp.int32, (n, n), 1)
    return s < r


def _sb_mask(z, mask):
    return z if mask is None else jnp.where(mask, z, SB_MASKED_SCORE)


def _sb_scores(z, nu, mask):
    z = _sb_mask(z, mask)
    nl = _softplus(z)
    log_beta = z - nl
    excl = _dot(nl.astype(BF16), nu)
    tot = excl[:, 0:1] - nl[:, 0:1]
    return log_beta, excl, tot


def _sb_weights(log_beta, excl, c):
    return jnp.exp(log_beta + excl + c).astype(BF16)


def _sb_prompt_kernel(q_ref, k_ref, v_ref, nu_ref, o_ref, acc_ref, c_ref, *, tq):
    qi = pl.program_id(2)
    nu = nu_ref[...]
    mask = _strict_lower(tq)
    q = q_ref[0]
    hsl = [slice(h * SB_HEAD_DIM, (h + 1) * SB_HEAD_DIM) for h in range(2)]

    def kv(kb):
        start = pl.multiple_of(kb * tq, tq)
        return k_ref[0, pl.ds(start, tq), :], v_ref[0, pl.ds(start, tq), :]

    def cmax():
        return jnp.maximum(jnp.max(c_ref[0]), jnp.max(c_ref[1]))

    @pl.when(qi == 0)
    def _():
        kd, vd = kv(qi)
        for h in range(2):
            lb, ex, tot = _sb_scores(_dot_nt(q[:, hsl[h]], kd[:, hsl[h]]), nu, mask)
            acc_ref[h] = _dot(_sb_weights(lb, ex, 0.0), vd[:, hsl[h]])
            c_ref[h] = tot

    @pl.when(qi > 0)
    def _():
        kd, vd = kv(qi)
        kp, vp = kv(qi - 1)
        for h in range(2):
            lb, ex, tot = _sb_scores(_dot_nt(q[:, hsl[h]], kd[:, hsl[h]]), nu, mask)
            lbp, exp_, totp = _sb_scores(_dot_nt(q[:, hsl[h]], kp[:, hsl[h]]), nu, None)
            o = _dot(_sb_weights(lb, ex, 0.0), vd[:, hsl[h]])
            o = o + _dot(_sb_weights(lbp, exp_, tot), vp[:, hsl[h]])
            acc_ref[h] = o
            c_ref[h] = tot + totp

    def cond(carry):
        kb, cm = carry
        return jnp.logical_and(kb >= 0, cm > SB_DEAD_LOG)

    def body(carry):
        kb, _ = carry
        kk, vv = kv(kb)
        for h in range(2):
            lb, ex, tot = _sb_scores(_dot_nt(q[:, hsl[h]], kk[:, hsl[h]]), nu, None)
            c = c_ref[h]
            acc_ref[h] += _dot(_sb_weights(lb, ex, c), vv[:, hsl[h]])
            c_ref[h] = c + tot
        return kb - 1, cmax()

    lax.while_loop(cond, body, (qi - 2, cmax()))
    o_ref[0] = jnp.concatenate([acc_ref[0], acc_ref[1]], axis=1).astype(BF16)


def _sb_near_kernel(q_ref, kd_ref, kp_ref, vd_ref, vp_ref, nu_ref, o_ref, flag_ref, *, tq, hb):
    qi = pl.program_id(2)
    nu = nu_ref[...]
    mask = _strict_lower(tq)
    q, kd, vd = q_ref[0], kd_ref[0], vd_ref[0]
    hs = [slice(h * SB_HEAD_DIM, (h + 1) * SB_HEAD_DIM) for h in range(hb)]

    @pl.when(qi == 0)
    def _():
        outs = []
        for h in range(hb):
            lb, ex, _ = _sb_scores(_dot_nt(q[:, hs[h]], kd[:, hs[h]]), nu, mask)
            outs.append(_dot(_sb_weights(lb, ex, 0.0), vd[:, hs[h]]))
        o_ref[0] = jnp.concatenate(outs, axis=1).astype(BF16)
        flag_ref[...] = jnp.full(flag_ref.shape, 2.0 * SB_DEAD_LOG, F32)

    @pl.when(qi > 0)
    def _():
        kp, vp = kp_ref[0], vp_ref[0]
        outs = []
        cm = None
        for h0 in range(0, hb, SB_STACK):
            group = range(h0, h0 + SB_STACK)
            zs, nls = [], []
            for h in group:
                z = _dot_nt(q[:, hs[h]], jnp.concatenate([kd[:, hs[h]], kp[:, hs[h]]], axis=0))
                zs += [_sb_mask(z[:, 0:tq], mask), z[:, tq:2 * tq]]
            nls = [_softplus(z) for z in zs]
            excl = _dot(jnp.concatenate(nls, axis=0).astype(BF16), nu)
            for i, h in enumerate(group):
                zd, zp, nld, nlp = zs[2 * i], zs[2 * i + 1], nls[2 * i], nls[2 * i + 1]
                exd, exq = excl[2 * i * tq:(2 * i + 1) * tq], excl[(2 * i + 1) * tq:(2 * i + 2) * tq]
                tot = exd[:, 0:1] - nld[:, 0:1]
                totp = exq[:, 0:1] - nlp[:, 0:1]
                a = jnp.concatenate([_sb_weights(zd - nld, exd, 0.0), _sb_weights(zp - nlp, exq, tot)], axis=1)
                outs.append(_dot(a, jnp.concatenate([vd[:, hs[h]], vp[:, hs[h]]], axis=0)))
                m = jnp.max(tot + totp)
                cm = m if cm is None else jnp.maximum(cm, m)
        o_ref[0] = jnp.concatenate(outs, axis=1).astype(BF16)
        flag_ref[...] = jnp.full(flag_ref.shape, jnp.where(qi > 1, cm, 2.0 * SB_DEAD_LOG), F32)


def _sb_prompt(q, k, v, *, tq):
    b, t, _ = q.shape
    hb = SB_PROMPT_HEADS
    ng = SB_HEADS // hb
    nq = t // tq
    lanes = hb * SB_HEAD_DIM
    cur = pl.BlockSpec((1, tq, lanes), lambda i, h, s: (i, s, h))
    prev = pl.BlockSpec((1, tq, lanes), lambda i, h, s: (i, jnp.maximum(s - 1, 0), h))
    o, flag = pl.pallas_call(
        functools.partial(_sb_near_kernel, tq=tq, hb=hb),
        grid=(b, ng, nq),
        in_specs=[cur, cur, prev, cur, prev, pl.BlockSpec((tq, tq), lambda i, h, s: (0, 0))],
        out_specs=[cur, pl.BlockSpec((1, 1, 1, SUBLANES, LANES), lambda i, h, s: (i, h, s, 0, 0))],
        out_shape=[jax.ShapeDtypeStruct((b, t, D_MODEL), BF16),
                   jax.ShapeDtypeStruct((b, ng, nq, SUBLANES, LANES), F32)],
        compiler_params=_params(("arbitrary", "arbitrary", "arbitrary")),
        name="sb_near",
    )(q, k, k, v, v, _neg_upper(tq))
    return lax.cond(jnp.max(flag) > SB_DEAD_LOG, lambda: _sb_prompt_full(q, k, v, tq=tq), lambda: o)


def _sb_prompt_full(q, k, v, *, tq):
    b, t, _ = q.shape
    hp = SB_HEADS // 2
    lanes = 2 * SB_HEAD_DIM
    nu = _neg_upper(tq)
    return pl.pallas_call(
        functools.partial(_sb_prompt_kernel, tq=tq),
        grid=(b, hp, t // tq),
        in_specs=[
            pl.BlockSpec((1, tq, lanes), lambda i, h, s: (i, s, h)),
            pl.BlockSpec((1, t, lanes), lambda i, h, s: (i, 0, h)),
            pl.BlockSpec((1, t, lanes), lambda i, h, s: (i, 0, h)),
            pl.BlockSpec((tq, tq), lambda i, h, s: (0, 0)),
        ],
        out_specs=pl.BlockSpec((1, tq, lanes), lambda i, h, s: (i, s, h)),
        out_shape=jax.ShapeDtypeStruct((b, t, D_MODEL), BF16),
        scratch_shapes=[pltpu.VMEM((2, tq, SB_HEAD_DIM), F32), pltpu.VMEM((2, tq, 1), F32)],
        compiler_params=_params(("arbitrary", "arbitrary", "arbitrary")),
        name="sb_prompt",
    )(q, k, v, nu)


def _sb_sample_kernel(q_ref, kn_ref, vn_ref, kc_ref, vc_ref, nun_ref, nup_ref, acc_ref, c_ref, flag_ref,
                      *, t, hb, nblk, tk):
    mask = _strict_lower(t)
    q, kn, vn = q_ref[0], kn_ref[0], vn_ref[0]
    hs = [slice(h * SB_HEAD_DIM, (h + 1) * SB_HEAD_DIM) for h in range(hb)]
    ks = [slice(kb * tk, (kb + 1) * tk) for kb in range(nblk - 1, -1, -1)]

    lbs, nls = [], []
    for h in range(hb):
        qh = q[:, hs[h]]
        zs = [_sb_mask(_dot_nt(qh, kn[:, hs[h]]), mask)]
        zs += [_dot(qh, kc_ref[0, h, :, s].astype(BF16)) for s in ks]
        nl = [_softplus(z) for z in zs]
        lbs.append([z - n for z, n in zip(zs, nl)])
        nls.append(nl)

    excl = []
    for sg in range(nblk + 1):
        stack = jnp.concatenate([nls[h][sg] for h in range(hb)], axis=0).astype(BF16)
        excl.append(_dot(stack, nun_ref[...] if sg == 0 else nup_ref[...]))

    cm = None
    for h in range(hb):
        rs = slice(h * t, (h + 1) * t)
        ex = excl[0][rs]
        o = _dot(_sb_weights(lbs[h][0], ex, 0.0), vn[:, hs[h]])
        c = ex[:, 0:1] - nls[h][0][:, 0:1]
        for i, s in enumerate(ks):
            ex = excl[i + 1][rs]
            o = o + _dot_nt(_sb_weights(lbs[h][i + 1], ex, c), vc_ref[0, h, :, s].astype(BF16))
            c = c + (ex[:, 0:1] - nls[h][i + 1][:, 0:1])
        acc_ref[0, :, hs[h]] = o
        c_ref[0, h] = jnp.broadcast_to(c, (t, LANES))
        m = jnp.max(c)
        cm = m if cm is None else jnp.maximum(cm, m)
    flag_ref[...] = jnp.full(flag_ref.shape, cm, F32)


def _sb_tail_kernel(q_ref, kc_ref, vc_ref, nu_ref, acc_in_ref, c_in_ref, acc_ref, acc_scr, c_scr):
    s = pl.program_id(2)

    @pl.when(s == 0)
    def _():
        for h in range(2):
            acc_scr[h] = acc_in_ref[0][:, h * SB_HEAD_DIM:(h + 1) * SB_HEAD_DIM]
            c_scr[h] = c_in_ref[0, h][:, 0:1]

    @pl.when(jnp.maximum(jnp.max(c_scr[0]), jnp.max(c_scr[1])) > SB_DEAD_LOG)
    def _():
        for h in range(2):
            qh = q_ref[0][:, h * SB_HEAD_DIM:(h + 1) * SB_HEAD_DIM]
            lb, ex, tot = _sb_scores(_dot(qh, kc_ref[0, h].astype(BF16)), nu_ref[...], None)
            c = c_scr[h]
            acc_scr[h] += _dot_nt(_sb_weights(lb, ex, c), vc_ref[0, h].astype(BF16))
            c_scr[h] = c + tot

    @pl.when(s == pl.num_programs(2) - 1)
    def _():
        acc_ref[0] = jnp.concatenate([acc_scr[0], acc_scr[1]], axis=1)


def _sb_sample(q, kn, vn, kct, vct):
    b, t, _ = q.shape
    past = kct.shape[3]
    tk = SB_BLOCK
    near = min(SB_SAMPLE_NEAR, past)
    hb = SB_SAMPLE_HEADS
    ng = SB_HEADS // hb
    new_spec = pl.BlockSpec((1, t, hb * SB_HEAD_DIM), lambda i, h: (i, 0, h))
    cache_spec = pl.BlockSpec((1, hb, SB_HEAD_DIM, near), lambda i, h: (i, h, 0, past // near - 1))
    nun, nup = _neg_upper(t), _neg_upper(tk)
    acc, c, flag = pl.pallas_call(
        functools.partial(_sb_sample_kernel, t=t, hb=hb, nblk=near // tk, tk=tk),
        grid=(b, ng),
        in_specs=[new_spec, new_spec, new_spec, cache_spec, cache_spec,
                  pl.BlockSpec((t, t), lambda i, h: (0, 0)),
                  pl.BlockSpec((tk, tk), lambda i, h: (0, 0))],
        out_specs=[new_spec,
                   pl.BlockSpec((1, hb, t, LANES), lambda i, h: (i, h, 0, 0)),
                   pl.BlockSpec((1, 1, SUBLANES, LANES), lambda i, h: (i, h, 0, 0))],
        out_shape=[jax.ShapeDtypeStruct((b, t, D_MODEL), F32),
                   jax.ShapeDtypeStruct((b, SB_HEADS, t, LANES), F32),
                   jax.ShapeDtypeStruct((b, ng, SUBLANES, LANES), F32)],
        compiler_params=_params(("arbitrary", "arbitrary")),
        name="sb_sample",
    )(q, kn, vn, kct, vct, nun, nup)
    nfar = (past - near) // tk
    if nfar == 0:
        return acc

    def tail():
        pair = pl.BlockSpec((1, t, 2 * SB_HEAD_DIM), lambda i, h, s: (i, 0, h))
        far = pl.BlockSpec((1, 2, SB_HEAD_DIM, tk), lambda i, h, s: (i, h, 0, nfar - 1 - s))
        return pl.pallas_call(
            _sb_tail_kernel,
            grid=(b, SB_HEADS // 2, nfar),
            in_specs=[pair, far, far, pl.BlockSpec((tk, tk), lambda i, h, s: (0, 0)), pair,
                      pl.BlockSpec((1, 2, t, LANES), lambda i, h, s: (i, h, 0, 0))],
            out_specs=pair,
            out_shape=jax.ShapeDtypeStruct((b, t, D_MODEL), F32),
            scratch_shapes=[pltpu.VMEM((2, t, SB_HEAD_DIM), F32), pltpu.VMEM((2, t, 1), F32)],
            compiler_params=_params(("arbitrary", "arbitrary", "arbitrary")),
            name="sb_tail",
        )(q, kct, vct, nup, acc, c)

    return lax.cond(jnp.max(flag) > SB_DEAD_LOG, tail, lambda: acc)


CONV_ROWS = 128
CONV_LANES = 128


def _mix_kernel(x_ref, osb_ref, left_ref, mk_ref, mv_ref, gpre_ref, wa_ref, wb_ref,
                wsb_ref, wcv_ref, wmo_ref, wout_ref, dww_ref, dwb_ref, lng_ref, lnb_ref, gpost_ref,
                y_ref, ulast_ref, hb_ref, uf_ref, cc_ref, qm_ref, om_ref, mg_ref, gb_ref, *, G, tt):
    s = pl.program_id(1)
    rows = G * tt
    H0 = CONV_HALO
    off = H0 - (CONV_WIDTH - 1)
    x = x_ref[...].reshape(rows, D_MODEL)
    hb_ref[...] = _rms(x, gpre_ref[...]).astype(BF16)

    @pl.when(s == 0)
    def _():
        uf_ref[:, 0:H0, :] = left_ref[...]

    @pl.when(s > 0)
    def _():
        uf_ref[:, 0:H0, :] = uf_ref[:, tt:tt + H0, :]

    pa = _dot(hb_ref[...], wa_ref[...])
    uf_ref[:, H0:H0 + tt, :] = (pa[:, 0:D_MODEL] * _sigmoid(pa[:, D_MODEL:2 * D_MODEL])).reshape(G, tt, D_MODEL)
    ulast_ref[...] = uf_ref[:, tt:tt + H0, :]

    qm_ref[...] = (pa[:, 2 * D_MODEL:] * (MEM_HEAD_DIM ** -0.5)).astype(BF16)
    gates = _dot(hb_ref[...], wb_ref[...])
    gb_ref[...] = _sigmoid(gates[:, D_MODEL:2 * D_MODEL])
    for g in range(G):
        for h in range(MEM_HEADS):
            cs = slice(h * MEM_HEAD_DIM, (h + 1) * MEM_HEAD_DIM)
            sc = _dot_nt(qm_ref[g * tt:(g + 1) * tt, cs], mk_ref[g, h].astype(BF16))
            p = jnp.exp(sc - jnp.max(sc, axis=-1, keepdims=True))
            den = jnp.sum(p, axis=-1, keepdims=True)
            oh = _dot(p.astype(BF16), mv_ref[g, h].astype(BF16)) * (1.0 / den)
            om_ref[g * tt:(g + 1) * tt, cs] = oh.astype(BF16)
    y_sb = _dot(osb_ref[...].reshape(rows, D_MODEL).astype(BF16), wsb_ref[...])
    mg_ref[...] = _sigmoid(gates[:, 0:D_MODEL]) * y_sb
    mg_ref[...] += _sigmoid(gates[:, 2 * D_MODEL:]) * _dot(om_ref[...], wmo_ref[...])

    rc = min(CONV_ROWS, tt)
    win = rc + H0
    for g in range(G):
        for r0 in range(0, tt, rc):
            for c0 in range(0, D_MODEL, CONV_LANES):
                cs = slice(c0, c0 + CONV_LANES)
                window = uf_ref[g, r0:r0 + win, cs]
                acc = jnp.broadcast_to(dwb_ref[:, cs], (rc, CONV_LANES))
                for r in range(SUBLANES):
                    shifted = window if r == 0 else pltpu.roll(window, win - r, axis=0)
                    for m in range(H0 // SUBLANES + 1):
                        j = SUBLANES * m + r - off
                        if 0 <= j < CONV_WIDTH:
                            acc = acc + shifted[SUBLANES * m:SUBLANES * m + rc] * dww_ref[j:j + 1, cs]
                cc_ref[g * tt + r0:g * tt + r0 + rc, cs] = acc

    cc = cc_ref[...]
    mu = jnp.mean(cc, axis=-1, keepdims=True)
    d = cc - mu
    var = jnp.mean(d * d, axis=-1, keepdims=True)
    yn = d * lax.rsqrt(var + NORM_EPS) * lng_ref[...] + lnb_ref[...]
    y_conv = _dot((yn * _sigmoid(yn)).astype(BF16), wcv_ref[...])

    merged = mg_ref[...] + gb_ref[...] * y_conv
    out = _dot(merged.astype(BF16), wout_ref[...])
    y_ref[...] = (x + _rms(out, gpost_ref[...])).reshape(G, tt, D_MODEL)


def _mix(x, osb, left, mk, mv, gpre, w_in, wsb, wcv, wmo, wout, dww, dwb, lng, lnb, gpost, *, G, tt):
    b, t, _ = x.shape
    rows = G * tt
    tok = pl.BlockSpec((G, tt, D_MODEL), lambda i, s: (i, s, 0))
    halo = pl.BlockSpec((G, CONV_HALO, D_MODEL), lambda i, s: (i, 0, 0))
    mem = pl.BlockSpec((G, MEM_HEADS, MEM_TOKENS, MEM_HEAD_DIM), lambda i, s: (i, 0, 0, 0))
    wsq = _resident((D_MODEL, D_MODEL), lambda i, s: (0, 0))
    vec = pl.BlockSpec((1, D_MODEL), lambda i, s: (0, 0))
    return pl.pallas_call(
        functools.partial(_mix_kernel, G=G, tt=tt),
        grid=(b // G, t // tt),
        in_specs=[tok, tok, halo, mem, mem, vec,
                  _resident((D_MODEL, 3 * D_MODEL), lambda i, s: (0, 1)),
                  _resident((D_MODEL, 3 * D_MODEL), lambda i, s: (0, 2)),
                  wsq, wsq, wsq, wsq,
                  pl.BlockSpec((CONV_WIDTH, D_MODEL), lambda i, s: (0, 0)),
                  vec, vec, vec, vec],
        out_specs=[tok, halo],
        out_shape=[jax.ShapeDtypeStruct((b, t, D_MODEL), F32),
                   jax.ShapeDtypeStruct((b, CONV_HALO, D_MODEL), F32)],
        scratch_shapes=[pltpu.VMEM((rows, D_MODEL), BF16),
                        pltpu.VMEM((G, CONV_HALO + tt, D_MODEL), F32),
                        pltpu.VMEM((rows, D_MODEL), F32),
                        pltpu.VMEM((rows, D_MODEL), BF16),
                        pltpu.VMEM((rows, D_MODEL), BF16),
                        pltpu.VMEM((rows, D_MODEL), F32),
                        pltpu.VMEM((rows, D_MODEL), F32)],
        compiler_params=_params(("arbitrary", "arbitrary")),
        name="mix",
    )(x, osb, left, mk, mv, gpre, w_in, w_in, wsb, wcv, wmo, wout, dww, dwb, lng, lnb, gpost)


FFN_ROWS = 256
FFN_COLS = 512


def _gelu_tanh(x):
    return 0.5 * x * (1.0 + jnp.tanh(0.7978845608028654 * (x + 0.044715 * (x * x * x))))


def _ffn_kernel(x_ref, lg_ref, lv_ref, gpre_ref, gpost_ref, wg_ref, wv_ref, dwg_ref, dwv_ref, wd_ref,
                y_ref, st_ref, hb_ref, acc_ref, upf_ref, carry_ref, *, G, tt, nf):
    s = pl.program_id(1)
    f = pl.program_id(2)
    rows = G * tt
    S = SUBLANES
    lefts = (lg_ref, lv_ref)
    ws = (wg_ref, wv_ref)
    dws = (dwg_ref, dwv_ref)

    @pl.when(f == 0)
    def _():
        hb_ref[...] = _rms(x_ref[...].reshape(rows, D_MODEL), gpre_ref[...]).astype(BF16)
        acc_ref[...] = jnp.zeros_like(acc_ref)

    for half in range(2):
        @pl.when(s == 0)
        def _():
            upf_ref[half, :, 0:S, :] = lefts[half][...]

        @pl.when(s > 0)
        def _():
            upf_ref[half, :, 0:S, :] = carry_ref[f, half]

    rc = min(FFN_ROWS, rows)
    gpc = max(rc // tt, 1)
    tc = min(rc, tt)
    tf = wd_ref.shape[0]
    for r0 in range(0, rows, rc):
        g0, t0 = r0 // tt, r0 % tt
        hb = hb_ref[r0:r0 + rc, :]
        acts = []
        for c0 in range(0, tf, FFN_COLS):
            cs = slice(c0, min(c0 + FFN_COLS, tf))
            conv = []
            for half in range(2):
                up = _dot(hb, ws[half][:, cs]).reshape(gpc, tc, -1)
                upf_ref[half, g0:g0 + gpc, S + t0:S + t0 + tc, cs] = up
                dw = dws[half]
                c = (up * dw[2:3, cs]
                     + upf_ref[half, g0:g0 + gpc, S + t0 - 1:S + t0 - 1 + tc, cs] * dw[1:2, cs]
                     + upf_ref[half, g0:g0 + gpc, S + t0 - 2:S + t0 - 2 + tc, cs] * dw[0:1, cs])
                conv.append(c.reshape(rc, -1))
            acts.append((_gelu_tanh(conv[0]) * conv[1]).astype(BF16))
        acc_ref[r0:r0 + rc, :] += _dot(jnp.concatenate(acts, axis=1), wd_ref[...])

    for half in range(2):
        last = upf_ref[half, :, tt:tt + S, :]
        carry_ref[f, half] = last
        st_ref[half, f] = last

    @pl.when(f == nf - 1)
    def _():
        y = x_ref[...].reshape(rows, D_MODEL) + _rms(acc_ref[...], gpost_ref[...])
        y_ref[...] = y.reshape(G, tt, D_MODEL)


def _ffn(x, left, gpre, gpost, wup, dw, wdown, *, G, tt, nf):
    b, t, _ = x.shape
    rows = G * tt
    tf = FFN_DIM // nf
    S = SUBLANES
    tok = pl.BlockSpec((G, tt, D_MODEL), lambda i, s, f: (i, s, 0))
    vec = pl.BlockSpec((1, D_MODEL), lambda i, s, f: (0, 0))
    wspec = _resident if nf == 1 else pl.BlockSpec
    y, st = pl.pallas_call(
        functools.partial(_ffn_kernel, G=G, tt=tt, nf=nf),
        grid=(b // G, t // tt, nf),
        in_specs=[tok,
                  pl.BlockSpec((G, S, tf), lambda i, s, f: (i, 0, f)),
                  pl.BlockSpec((G, S, tf), lambda i, s, f: (i, 0, nf + f)),
                  vec, vec,
                  wspec((D_MODEL, tf), lambda i, s, f: (0, f)),
                  wspec((D_MODEL, tf), lambda i, s, f: (0, nf + f)),
                  pl.BlockSpec((FFN_CONV_WIDTH, tf), lambda i, s, f: (0, f)),
                  pl.BlockSpec((FFN_CONV_WIDTH, tf), lambda i, s, f: (0, nf + f)),
                  wspec((tf, D_MODEL), lambda i, s, f: (f, 0))],
        out_specs=[tok, pl.BlockSpec((2, nf, G, S, tf), lambda i, s, f: (0, 0, i, 0, 0))],
        out_shape=[jax.ShapeDtypeStruct((b, t, D_MODEL), F32),
                   jax.ShapeDtypeStruct((2, nf, b, S, tf), F32)],
        scratch_shapes=[pltpu.VMEM((rows, D_MODEL), BF16),
                        pltpu.VMEM((rows, D_MODEL), F32),
                        pltpu.VMEM((2, G, S + tt, tf), F32),
                        pltpu.VMEM((nf, 2, G, S, tf), F32)],
        compiler_params=_params(("arbitrary", "arbitrary", "arbitrary")),
        name="ffn",
    )(x, left, left, gpre, gpost, wup, wup, dw, dw, wdown)
    keep = FFN_CONV_WIDTH - 1
    state = st[:, :, :, S - keep:, :].transpose(2, 3, 0, 1, 4).reshape(b, keep, 2 * FFN_DIM)
    return y, state


def _tiles(b, t):
    if t >= 512:
        return {"qkv": (1, 1024), "mix": (1, 512), "ffn": (1, 512)}
    return {"qkv": (min(b, 512 // t), t), "mix": (min(b, 256 // t), t), "ffn": (min(b, 512 // t), t)}


def _layer(x, mk, mv, sb_cache, conv_left, ffn_left, w):
    b, t, _ = x.shape
    tiles = _tiles(b, t)
    prompt = sb_cache is None
    G, tt = tiles["qkv"]
    q, k, v, kf, vf = _qkv(x, w["g_mix_pre"], w["w_in"], G=G, tt=tt, dmajor=prompt)
    if prompt:
        osb = _sb_prompt(q, k, v, tq=SB_BLOCK)
        kf = kf.reshape(b, SB_HEADS, SB_HEAD_DIM, t).swapaxes(2, 3)
        vf = vf.reshape(b, SB_HEADS, SB_HEAD_DIM, t).swapaxes(2, 3)
    else:
        osb = _sb_sample(q, k, v, sb_cache[0].swapaxes(2, 3), sb_cache[1].swapaxes(2, 3))

    keep = CONV_WIDTH - 1
    left = jnp.pad(conv_left, ((0, 0), (CONV_HALO - keep, 0), (0, 0)))
    G, tt = tiles["mix"]
    x1, ulast = _mix(x, osb, left, mk, mv, w["g_mix_pre"], w["w_in"], w["w_sb_o"], w["w_conv_o"], w["w_mem_o"],
                     w["w_out"], w["conv_dw_w"], w["conv_dw_b"], w["conv_ln_g"], w["conv_ln_b"], w["g_mix_post"],
                     G=G, tt=tt)
    conv_state = ulast[:, CONV_HALO - keep:]

    fkeep = FFN_CONV_WIDTH - 1
    fleft = jnp.pad(ffn_left, ((0, 0), (SUBLANES - fkeep, 0), (0, 0)))
    G, tt = tiles["ffn"]
    y, ffn_state = _ffn(x1, fleft, w["g_ffn_pre"], w["g_ffn_post"], w["w_ffn_up"], w["ffn_dw_w"],
                        w["w_ffn_down"], G=G, tt=tt, nf=1)
    return y, kf, vf, conv_state, ffn_state


def kernel(x_prompt, x_sample, mem_prompt, cache_sb_k, cache_sb_v, state_conv, state_ffn_conv, cache_mem_k, cache_mem_v, g_mem, w_mem_kv, g_mix_pre, g_mix_post, w_in, w_sb_o, conv_dw_w, conv_dw_b, conv_ln_g, conv_ln_b, w_conv_o, w_mem_o, w_out, g_ffn_pre, g_ffn_post, w_ffn_up, ffn_dw_w, w_ffn_down):
    depth = w_in.shape[0]
    bp = x_prompt.shape[0]
    yp, ys = x_prompt, x_sample
    outs = [[] for _ in range(10)]
    for l in range(depth):
        vec = lambda a: a[l].reshape(1, -1)
        w = {
            "g_mix_pre": vec(g_mix_pre), "g_mix_post": vec(g_mix_post),
            "w_in": w_in[l].astype(BF16), "w_sb_o": w_sb_o[l].astype(BF16),
            "conv_dw_w": conv_dw_w[l], "conv_dw_b": vec(conv_dw_b),
            "conv_ln_g": vec(conv_ln_g), "conv_ln_b": vec(conv_ln_b),
            "w_conv_o": w_conv_o[l].astype(BF16), "w_mem_o": w_mem_o[l].astype(BF16),
            "w_out": w_out[l].astype(BF16),
            "g_ffn_pre": vec(g_ffn_pre), "g_ffn_post": vec(g_ffn_post),
            "w_ffn_up": w_ffn_up[l].astype(BF16), "ffn_dw_w": ffn_dw_w[l],
            "w_ffn_down": w_ffn_down[l].astype(BF16),
        }
        mkv, mkv_bf = _memkv(mem_prompt, vec(g_mem), w_mem_kv[l].astype(BF16))
        yp, kp, vp, cp, fp = _layer(
            yp, mkv_bf[0], mkv_bf[1], None,
            jnp.zeros((bp, CONV_WIDTH - 1, D_MODEL), F32),
            jnp.zeros((bp, FFN_CONV_WIDTH - 1, 2 * FFN_DIM), F32), w)
        ys, ks, vs, cs, fs = _layer(
            ys, cache_mem_k[l], cache_mem_v[l], (cache_sb_k[l], cache_sb_v[l]),
            state_conv[l], state_ffn_conv[l], w)
        for lst, val in zip(outs, (kp, vp, ks, vs, cp, cs, fp, fs, mkv[0], mkv[1])):
            lst.append(val)
    return (yp, ys) + tuple(jnp.stack(o) for o in outs)
```
